```python
import jax, jax.numpy as jnp
from jax import lax
import numpy as np

D_MODEL = 2048
BATCH = 4
SEQ = 2048
DEPTH = 1
DEC_BATCH = 128
DEC_SEQ = 1
PAST_LEN = 16384
PAGE_SIZE = 128

N_META = 16
H_RET = 8
DK_RET = D_MODEL // H_RET
DV_RET = D_MODEL // H_RET
DK_HG = 128
H_HG = D_MODEL // DK_HG
DV_HG = D_MODEL // H_HG
D_FF = ((8 * D_MODEL // 3 + 127) // 128) * 128
CHUNK_RET = 128
CHUNK_HG = 64
N_PROJ = 10
ALPHA = (2 * DEPTH) ** 0.25
BETA = (8 * DEPTH) ** -0.25
LN_EPS = 1e-5
ROPE_BASE = 10000.0

kernel_name = "hybrid_retention_hgrn2_macaron_step"


def layer_norm(x, g, b):
    xf = x.astype(jnp.float32)
    mu = jnp.mean(xf, axis=-1, keepdims=True)
    var = jnp.mean(jnp.square(xf - mu), axis=-1, keepdims=True)
    return ((xf - mu) * lax.rsqrt(var + LN_EPS)).astype(x.dtype) * g + b


def head_layer_norm(o):
    of = o.astype(jnp.float32)
    mu = jnp.mean(of, axis=-1, keepdims=True)
    var = jnp.mean(jnp.square(of - mu), axis=-1, keepdims=True)
    return ((of - mu) * lax.rsqrt(var + LN_EPS)).astype(o.dtype)


def head_rms_norm(o, g):
    of = o.astype(jnp.float32)
    ms = jnp.mean(jnp.square(of), axis=-1, keepdims=True)
    return (of * lax.rsqrt(ms + LN_EPS)).astype(o.dtype) * g


def swiglu(x, w_gate, w_up, w_down):
    return (jax.nn.silu(x @ w_gate) * (x @ w_up)) @ w_down


def rotary(x, pos):
    d = x.shape[-1]
    inv = ROPE_BASE ** (-jnp.arange(0, d, 2, dtype=jnp.float32) / d)
    ang = pos.astype(jnp.float32)[:, None] * inv[None, :]
    cos = jnp.cos(ang)[None, :, None, :].astype(x.dtype)
    sin = jnp.sin(ang)[None, :, None, :].astype(x.dtype)
    x1, x2 = x[..., 0::2], x[..., 1::2]
    return jnp.stack([x1 * cos - x2 * sin, x1 * sin + x2 * cos], axis=-1).reshape(x.shape)


def retention_log_decay():
    return jnp.log(1.0 - 2.0 ** (-5.0 - jnp.arange(H_RET, dtype=jnp.float32)))


def retention_chunk(S, q, k, v):
    C = q.shape[2]
    lg = retention_log_decay()
    idx = jnp.arange(C, dtype=jnp.float32)
    rel = idx[:, None] - idx[None, :]
    D = jnp.where(rel >= 0, jnp.exp(lg[:, None, None] * jnp.maximum(rel, 0.0)), 0.0).astype(q.dtype)
    inner = jnp.einsum('bhid,bhjd->bhij', q, k) * D
    q_dec = q * jnp.exp(lg[:, None] * (idx + 1.0))[:, :, None].astype(q.dtype)
    o = jnp.einsum('bhij,bhjv->bhiv', inner, v) + jnp.einsum('bhid,bhdv->bhiv', q_dec, S)
    k_dec = k * jnp.exp(lg[:, None] * (C - 1.0 - idx))[:, :, None].astype(k.dtype)
    S_new = jnp.exp(lg * C)[:, None, None].astype(S.dtype) * S + jnp.einsum('bhjd,bhjv->bhdv', k_dec, v)
    return S_new.astype(S.dtype), o


def hgrn2_chunk(S, q, k, v, logf):
    C = q.shape[2]
    b = jnp.cumsum(logf.astype(jnp.float32), axis=2)
    causal = jnp.tril(jnp.ones((C, C), dtype=bool))
    diff = b[:, :, :, None, :] - b[:, :, None, :, :]
    decay = jnp.exp(jnp.where(causal[None, None, :, :, None], diff, -jnp.inf)).astype(q.dtype)
    A = jnp.einsum('bhid,bhjd,bhijd->bhij', q, k, decay)
    o = jnp.einsum('bhij,bhjv->bhiv', A, v) + jnp.einsum('bhid,bhdv->bhiv', q * jnp.exp(b).astype(q.dtype), S)
    b_last = b[:, :, -1:, :]
    k_dec = k * jnp.exp(b_last - b).astype(k.dtype)
    S_new = jnp.exp(b_last[:, :, 0, :, None]) * S + jnp.einsum('bhjd,bhjv->bhdv', k_dec, v)
    return S_new.astype(S.dtype), o


def chunked_scan(chunk_fn, S0, seqs, chunk):
    S, o_meta = chunk_fn(S0, *[a[:, :, :N_META] for a in seqs])
    rest = [a[:, :, N_META:] for a in seqs]
    n = rest[0].shape[2] // chunk

    def to_chunks(a):
        B, H, L, d = a.shape
        return jnp.moveaxis(a.reshape(B, H, n, chunk, d), 2, 0)

    def step(carry, xs):
        return chunk_fn(carry, *xs)

    S, o_rest = lax.scan(step, S, tuple(to_chunks(a) for a in rest))
    B, H = o_rest.shape[1], o_rest.shape[2]
    o_rest = jnp.moveaxis(o_rest, 0, 2).reshape(B, H, n * chunk, o_rest.shape[-1])
    return S, jnp.concatenate([o_meta, o_rest], axis=2)


def token_mixer(x, pos, s_ret, s_hg, prompt, w_in, lb, hg_norm_g, w_out):
    B, L, _ = x.shape
    q_r, k_r, v_r, g_r, q_h, f_h, i_h, g_h, a_r, a_h = jnp.split(x @ w_in, N_PROJ, axis=-1)

    def to_heads(a, h):
        return jnp.swapaxes(a.reshape(B, L, h, -1), 1, 2)

    q_r = jnp.swapaxes(rotary(q_r.reshape(B, L, H_RET, DK_RET), pos), 1, 2)
    k_r = jnp.swapaxes(rotary(k_r.reshape(B, L, H_RET, DK_RET), pos), 1, 2) * (DK_RET ** -0.5)
    v_r = to_heads(v_r, H_RET)
    f = lb + (1.0 - lb) * jax.nn.sigmoid(f_h.astype(jnp.float32))
    logf = to_heads(jnp.log(f), H_HG)
    k_h = to_heads((1.0 - f).astype(x.dtype), H_HG)
    q_h = to_heads(jax.nn.silu(q_h), H_HG)
    i_h = to_heads(i_h, H_HG)

    if prompt:
        s_ret, o_r = chunked_scan(retention_chunk, s_ret, (q_r, k_r, v_r), CHUNK_RET)
        s_hg, o_h = chunked_scan(hgrn2_chunk, s_hg, (q_h, k_h, i_h, logf), CHUNK_HG)
    else:
        s_ret, o_r = retention_chunk(s_ret, q_r, k_r, v_r)
        s_hg, o_h = hgrn2_chunk(s_hg, q_h, k_h, i_h, logf)

    o_r = head_layer_norm(jnp.swapaxes(o_r, 1, 2)).reshape(B, L, D_MODEL) * jax.nn.silu(g_r)
    o_h = head_rms_norm(jnp.swapaxes(o_h, 1, 2), hg_norm_g.reshape(H_HG, DV_HG)).reshape(B, L, D_MODEL) * jax.nn.silu(g_h)
    y = jax.nn.sigmoid(a_r) * o_r + jax.nn.sigmoid(a_h) * o_h
    return y @ w_out, s_ret, s_hg


def decoder_layer(x, pos, s_ret, s_hg, prompt, ln1_g, ln1_b, f1_g, f1_u, f1_d, w_in, lb, hg_norm_g,
                  w_out, ln2_g, ln2_b, f2_g, f2_u, f2_d, ln3_g, ln3_b):
    x = layer_norm(ALPHA * x + 0.5 * swiglu(x, f1_g, f1_u, f1_d), ln1_g, ln1_b)
    m, s_ret, s_hg = token_mixer(x, pos, s_ret, s_hg, prompt, w_in, lb, hg_norm_g, w_out)
    x = layer_norm(ALPHA * x + m, ln2_g, ln2_b)
    x = layer_norm(ALPHA * x + 0.5 * swiglu(x, f2_g, f2_u, f2_d), ln3_g, ln3_b)
    return x, s_ret, s_hg


def setup_inputs(seed: int = 0) -> dict:
    key = jax.random.key(seed)
    ks = jax.random.split(key, 21)
    f32 = jnp.float32
    sD = D_MODEL ** -0.5
    sF = D_FF ** -0.5

    def nrm(k, shape, scale):
        return jax.random.normal(k, shape, f32) * scale

    return {
        "x_prompt": nrm(ks[0], (BATCH, SEQ, D_MODEL), 1.0),
        "x_sample": nrm(ks[1], (DEC_BATCH, DEC_SEQ, D_MODEL), 1.0),
        "state_ret": nrm(ks[2], (DEPTH, DEC_BATCH, H_RET, DK_RET, DV_RET), 0.5),
        "state_hgrn": nrm(ks[3], (DEPTH, DEC_BATCH, H_HG, DK_HG, DV_HG), 0.5),
        "meta_tokens": nrm(ks[4], (N_META, D_MODEL), 1.0),
        "ln1_g": 1.0 + nrm(ks[5], (DEPTH, D_MODEL), 0.02),
        "ln1_b": nrm(ks[6], (DEPTH, D_MODEL), 0.02),
        "ffn1_w_gate": nrm(ks[7], (DEPTH, D_MODEL, D_FF), sD),
        "ffn1_w_up": nrm(ks[8], (DEPTH, D_MODEL, D_FF), sD),
        "ffn1_w_down": nrm(ks[9], (DEPTH, D_FF, D_MODEL), sF * BETA),
        "w_in": nrm(ks[10], (DEPTH, D_MODEL, N_PROJ * D_MODEL), sD),
        "hgrn_lb_logits": nrm(ks[11], (DEPTH + 1, D_MODEL), 0.1),
        "hgrn_norm_g": 1.0 + nrm(ks[12], (DEPTH, D_MODEL), 0.02),
        "w_out": nrm(ks[13], (DEPTH, D_MODEL, D_MODEL), sD * BETA),
        "ln2_g": 1.0 + nrm(ks[14], (DEPTH, D_MODEL), 0.02),
        "ln2_b": nrm(ks[15], (DEPTH, D_MODEL), 0.02),
        "ffn2_w_gate": nrm(ks[16], (DEPTH, D_MODEL, D_FF), sD),
        "ffn2_w_up": nrm(ks[17], (DEPTH, D_MODEL, D_FF), sD),
        "ffn2_w_down": nrm(ks[18], (DEPTH, D_FF, D_MODEL), sF * BETA),
        "ln3_g": 1.0 + nrm(ks[19], (DEPTH, D_MODEL), 0.02),
        "ln3_b": nrm(ks[20], (DEPTH, D_MODEL), 0.02),
    }


def reference(x_prompt, x_sample, state_ret, state_hgrn, meta_tokens, ln1_g, ln1_b, ffn1_w_gate,
              ffn1_w_up, ffn1_w_down, w_in, hgrn_lb_logits, hgrn_norm_g, w_out, ln2_g, ln2_b,
              ffn2_w_gate, ffn2_w_up, ffn2_w_down, ln3_g, ln3_b):
    B = x_prompt.shape[0]
    dt = x_prompt.dtype
    lb_all = jnp.cumsum(jax.nn.softmax(hgrn_lb_logits.astype(jnp.float32), axis=0), axis=0)

    meta = jnp.broadcast_to(meta_tokens.astype(dt)[None], (B, N_META, D_MODEL))
    hp = jnp.concatenate([meta, x_prompt], axis=1)
    hs = x_sample
    pos_p = jnp.arange(N_META + SEQ, dtype=jnp.int32)
    pos_s = PAST_LEN + jnp.arange(DEC_SEQ, dtype=jnp.int32)

    rp_list, rs_list, gp_list, gs_list = [], [], [], []
    for l in range(DEPTH):
        lp = (ln1_g[l], ln1_b[l], ffn1_w_gate[l], ffn1_w_up[l], ffn1_w_down[l], w_in[l], lb_all[l],
              hgrn_norm_g[l], w_out[l], ln2_g[l], ln2_b[l], ffn2_w_gate[l], ffn2_w_up[l],
              ffn2_w_down[l], ln3_g[l], ln3_b[l])
        s_ret0 = jnp.zeros((B, H_RET, DK_RET, DV_RET), dt)
        s_hg0 = jnp.zeros((B, H_HG, DK_HG, DV_HG), dt)
        hp, rp, gp = decoder_layer(hp, pos_p, s_ret0, s_hg0, True, *lp)
        hs, rs, gs = decoder_layer(hs, pos_s, state_ret[l], state_hgrn[l], False, *lp)
        rp_list.append(rp)
        rs_list.append(rs)
        gp_list.append(gp)
        gs_list.append(gs)

    y_prompt = hp[:, N_META:]
    y_sample = hs
    state_ret_prompt = jnp.stack(rp_list, axis=0)
    state_ret_sample = jnp.stack(rs_list, axis=0)
    state_hgrn_prompt = jnp.stack(gp_list, axis=0)
    state_hgrn_sample = jnp.stack(gs_list, axis=0)
    return (y_prompt, y_sample, state_ret_prompt, state_ret_sample, state_hgrn_prompt, state_hgrn_sample)
```

```python
import functools
import math

import jax
import jax.numpy as jnp
from jax import lax
from jax.experimental import pallas as pl
from jax.experimental.pallas import tpu as pltpu

F32 = jnp.float32
BF16 = jnp.bfloat16

PAST_LEN = 16384
LN_EPS = 1e-5
ROPE_BASE = 10000.0
N_PROJ = 10

V7X_VMEM_BYTES = 64 * 1024 * 1024
VMEM_LIMIT_BYTES = V7X_VMEM_BYTES - 8 * 1024 * 1024
LANES = 128
BF16_SUBLANES = 16

RET_CHUNK = 256
HGRN_CHUNK = 128


def _params(*sem):
    return pltpu.CompilerParams(dimension_semantics=sem, vmem_limit_bytes=VMEM_LIMIT_BYTES)


def _dot(a, b):
    return jnp.dot(a, b, preferred_element_type=F32)


def _dot_nt(a, b):
    return lax.dot_general(a, b, (((1,), (1,)), ((), ())), preferred_element_type=F32)


def _dot_tn(a, b):
    return lax.dot_general(a, b, (((0,), (0,)), ((), ())), preferred_element_type=F32)


def _sigmoid(x):
    return jax.nn.sigmoid(x)


def _silu(x):
    return x * jax.nn.sigmoid(x)


def _layer_norm(y, g, b):
    mu = jnp.mean(y, axis=-1, keepdims=True)
    d = y - mu
    var = jnp.mean(d * d, axis=-1, keepdims=True)
    return d * lax.rsqrt(var + LN_EPS) * g + b


def _pick_tile(n, target, quantum):
    if n <= target:
        return n
    t = (target // quantum) * quantum
    while t > quantum and n % t:
        t -= quantum
    assert n % t == 0, (n, target, quantum)
    return t


def _ffn_ln_kernel(x_ref, wg_ref, wu_ref, wd_ref, g_ref, b_ref, *rest, alpha, n_ff, emit_bf16):
    if emit_bf16:
        o_ref, ob_ref, xb_scr = rest
    else:
        (o_ref, xb_scr), ob_ref = rest, None
    j = pl.program_id(1)

    @pl.when(j == 0)
    def _():
        xb_scr[...] = x_ref[...].astype(BF16)
        o_ref[...] = jnp.zeros_like(o_ref)

    xb = xb_scr[...]
    gate = _dot(xb, wg_ref[...])
    up = _dot(xb, wu_ref[...])
    h = (_silu(gate) * up).astype(BF16)
    o_ref[...] += _dot(h, wd_ref[...])

    @pl.when(j == n_ff - 1)
    def _():
        y = _layer_norm(alpha * x_ref[...] + 0.5 * o_ref[...], g_ref[...], b_ref[...])
        o_ref[...] = y
        if emit_bf16:
            ob_ref[...] = y.astype(BF16)


def _ffn_ln(x, wg, wu, wd, ln_g, ln_b, *, alpha, emit_bf16, name):
    rows, d = x.shape
    fpad = wg.shape[1]
    tm = _pick_tile(rows, 512, BF16_SUBLANES)
    tf = _pick_tile(fpad, 512, LANES)
    n_ff = fpad // tf
    out_shape = [jax.ShapeDtypeStruct((rows, d), F32)]
    out_specs = [pl.BlockSpec((tm, d), lambda i, j: (i, 0))]
    if emit_bf16:
        out_shape.append(jax.ShapeDtypeStruct((rows, d), BF16))
        out_specs.append(pl.BlockSpec((tm, d), lambda i, j: (i, 0)))
    res = pl.pallas_call(
        functools.partial(_ffn_ln_kernel, alpha=alpha, n_ff=n_ff, emit_bf16=emit_bf16),
        grid=(rows // tm, n_ff),
        in_specs=[
            pl.BlockSpec((tm, d), lambda i, j: (i, 0)),
            pl.BlockSpec((d, tf), lambda i, j: (0, j)),
            pl.BlockSpec((d, tf), lambda i, j: (0, j)),
            pl.BlockSpec((tf, d), lambda i, j: (j, 0)),
            pl.BlockSpec((1, d), lambda i, j: (0, 0)),
            pl.BlockSpec((1, d), lambda i, j: (0, 0)),
        ],
        out_specs=out_specs,
        out_shape=out_shape,
        scratch_shapes=[pltpu.VMEM((tm, d), BF16)],
        compiler_params=_params("arbitrary", "arbitrary"),
        name=name,
    )(x, wg, wu, wd, ln_g, ln_b)
    return res if emit_bf16 else res[0]


def _proj_kernel(x_ref, w_ref, *rest, epilogue, dk, nq_tiles, k_scale):
    o_ref = rest[-1]
    acc = _dot(x_ref[...], w_ref[...])
    if epilogue == "none":
        o_ref[...] = acc.astype(o_ref.dtype)
    elif epilogue == "silu":
        o_ref[...] = _silu(acc).astype(o_ref.dtype)
    elif epilogue == "fgate":
        lb = rest[0][...]
        o_ref[...] = (lb + (1.0 - lb) * _sigmoid(acc)).astype(o_ref.dtype)
    else:
        cos = rest[0][...]
        sin = rest[1][...]
        scale = jnp.where(pl.program_id(0) < nq_tiles, 1.0, k_scale).astype(F32)
        even = (lax.broadcasted_iota(jnp.int32, cos.shape, 1) & 1) == 0
        for hh in range(acc.shape[1] // dk):
            a = acc[:, hh * dk:(hh + 1) * dk]
            swapped = jnp.where(even, pltpu.roll(a, dk - 1, 1), pltpu.roll(a, 1, 1))
            o_ref[:, hh * dk:(hh + 1) * dk] = ((a * cos + swapped * sin) * scale).astype(o_ref.dtype)


def _proj(xb, w, col0, ncols, *, epilogue, out_dtype, name, extras=(), dk=0, k_scale=1.0):
    rows, d = xb.shape
    tm = _pick_tile(extras[0].shape[0] if epilogue == "rot" else rows, 1024, BF16_SUBLANES)
    tn = _pick_tile(ncols, 1024, max(LANES, dk))
    assert col0 % tn == 0 and rows % tm == 0
    in_specs = [
        pl.BlockSpec((tm, d), lambda n, m: (m, 0)),
        pl.BlockSpec((d, tn), lambda n, m: (0, col0 // tn + n)),
    ]
    if epilogue == "rot":
        cos, sin = extras
        tab_blocks = cos.shape[0] // tm
        in_specs += [pl.BlockSpec((tm, dk), lambda n, m: (m % tab_blocks, 0))] * 2
    elif epilogue == "fgate":
        in_specs += [pl.BlockSpec((1, tn), lambda n, m: (0, n))]
    return pl.pallas_call(
        functools.partial(_proj_kernel, epilogue=epilogue, dk=dk,
                          nq_tiles=(ncols // 2) // tn, k_scale=k_scale),
        grid=(ncols // tn, rows // tm),
        in_specs=in_specs,
        out_specs=pl.BlockSpec((tm, tn), lambda n, m: (m, n)),
        out_shape=jax.ShapeDtypeStruct((rows, ncols), out_dtype),
        compiler_params=_params("arbitrary", "arbitrary"),
        name=name,
    )(xb, w, *extras)


def _project_all(xb, w_in, d, cos, sin, lb, dk_ret, out_dtype, tag):
    qk = _proj(xb, w_in, 0, 2 * d, epilogue="rot", out_dtype=out_dtype, extras=(cos, sin),
               dk=dk_ret, k_scale=dk_ret ** -0.5, name=f"proj_qk_{tag}")
    vg = _proj(xb, w_in, 2 * d, 2 * d, epilogue="none", out_dtype=out_dtype, name=f"proj_vg_{tag}")
    qh = _proj(xb, w_in, 4 * d, d, epilogue="silu", out_dtype=out_dtype, name=f"proj_qh_{tag}")
    fg = _proj(xb, w_in, 5 * d, d, epilogue="fgate", out_dtype=F32, extras=(lb,), name=f"proj_f_{tag}")
    rest = _proj(xb, w_in, 6 * d, 4 * d, epilogue="none", out_dtype=out_dtype, name=f"proj_rest_{tag}")
    return qk, vg, qh, fg, rest


def _ret_log_gamma(h):
    return math.log(1.0 - 2.0 ** (-5.0 - h))


def _ret_scan_kernel(q_ref, k_ref, v_ref, g_ref, a_ref, km_ref, vm_ref, y_ref, s_ref, s_scr, *,
                     n_heads, dk, dv, chunk, n_meta, n_chunks):
    c = pl.program_id(1)

    @pl.when(c == 0)
    def _():
        mpos = lax.broadcasted_iota(jnp.int32, (n_meta, dk), 0).astype(F32)
        for h in range(n_heads):
            lg = _ret_log_gamma(h)
            kd = (km_ref[:, h * dk:(h + 1) * dk].astype(F32) * jnp.exp(lg * (n_meta - 1.0 - mpos))).astype(BF16)
            s_scr[h] = _dot_tn(kd, vm_ref[:, h * dv:(h + 1) * dv].astype(BF16))

    rel = (lax.broadcasted_iota(jnp.int32, (chunk, chunk), 0)
           - lax.broadcasted_iota(jnp.int32, (chunk, chunk), 1)).astype(F32)
    pos_k = lax.broadcasted_iota(jnp.int32, (chunk, dk), 0).astype(F32)
    pos_v = lax.broadcasted_iota(jnp.int32, (chunk, dv), 0).astype(F32)
    for h in range(n_heads):
        lg = _ret_log_gamma(h)
        ks = slice(h * dk, (h + 1) * dk)
        vs = slice(h * dv, (h + 1) * dv)
        q = q_ref[:, ks]
        k = k_ref[:, ks]
        v = v_ref[:, vs]
        s = s_scr[h]
        decay = jnp.where(rel >= 0, jnp.exp(lg * jnp.maximum(rel, 0.0)), 0.0)
        inner = (_dot_nt(q, k) * decay).astype(BF16)
        o = _dot(inner, v) + _dot(q, s.astype(BF16)) * jnp.exp(lg * (pos_v + 1.0))
        kd = (k.astype(F32) * jnp.exp(lg * (chunk - 1.0 - pos_k))).astype(BF16)
        s_scr[h] = math.exp(lg * chunk) * s + _dot_tn(kd, v)
        mu = jnp.mean(o, axis=-1, keepdims=True)
        dlt = o - mu
        var = jnp.mean(dlt * dlt, axis=-1, keepdims=True)
        gates = _silu(g_ref[:, vs].astype(F32)) * _sigmoid(a_ref[:, vs].astype(F32))
        y_ref[:, vs] = (dlt * lax.rsqrt(var + LN_EPS) * gates).astype(y_ref.dtype)

    @pl.when(c == n_chunks - 1)
    def _():
        s_ref[0, 0] = s_scr[...]


def _ret_scan(qk, vg, rest, qk_s, vg_s, *, batch, seq, n_heads, dk, dv, n_meta, meta_row0, depth):
    d = n_heads * dk
    chunk = _pick_tile(seq, RET_CHUNK, BF16_SUBLANES)
    n_chunks = seq // chunk
    assert meta_row0 % n_meta == 0
    mblk = meta_row0 // n_meta
    row = lambda b, c: b * n_chunks + c
    return pl.pallas_call(
        functools.partial(_ret_scan_kernel, n_heads=n_heads, dk=dk, dv=dv, chunk=chunk,
                          n_meta=n_meta, n_chunks=n_chunks),
        grid=(batch, n_chunks),
        in_specs=[
            pl.BlockSpec((chunk, d), lambda b, c: (row(b, c), 0)),
            pl.BlockSpec((chunk, d), lambda b, c: (row(b, c), 1)),
            pl.BlockSpec((chunk, d), lambda b, c: (row(b, c), 0)),
            pl.BlockSpec((chunk, d), lambda b, c: (row(b, c), 1)),
            pl.BlockSpec((chunk, d), lambda b, c: (row(b, c), 2)),
            pl.BlockSpec((n_meta, d), lambda b, c: (mblk, 1)),
            pl.BlockSpec((n_meta, d), lambda b, c: (mblk, 0)),
        ],
        out_specs=[
            pl.BlockSpec((chunk, d), lambda b, c: (row(b, c), 0)),
            pl.BlockSpec((1, 1, n_heads, dk, dv), lambda b, c: (0, b, 0, 0, 0)),
        ],
        out_shape=[
            jax.ShapeDtypeStruct((batch * seq, d), BF16),
            jax.ShapeDtypeStruct((depth, batch, n_heads, dk, dv), F32),
        ],
        scratch_shapes=[pltpu.VMEM((n_heads, dk, dv), F32)],
        compiler_params=_params("arbitrary", "arbitrary"),
        name="ret_scan",
    )(qk, qk, vg, vg, rest, qk_s, vg_s)


def _split_dot(m01, x):
    hi = x.astype(BF16)
    r1 = x - hi.astype(F32)
    mid = r1.astype(BF16)
    lo = (r1 - mid.astype(F32)).astype(BF16)
    return _dot(m01, hi) + _dot(m01, mid) + _dot(m01, lo)


def _level_boundary(b, level, rows):
    n, width = b.shape
    m = 1 << level
    half = m >> 1
    if m >= 8:
        pieces = [jnp.broadcast_to(b[p * m + half - 1:p * m + half, :], (m, width)) for p in range(n // m)]
        return pieces[0] if len(pieces) == 1 else jnp.concatenate(pieces, axis=0)
    if m == 2:
        return jnp.where((rows & 1) == 1, pltpu.roll(b, 1, 0), b)
    r4 = rows & 3
    nxt = pltpu.roll(b, n - 1, 0)
    prv = pltpu.roll(b, 1, 0)
    prv2 = pltpu.roll(b, 2, 0)
    return jnp.where(r4 == 0, nxt, jnp.where(r4 == 1, b, jnp.where(r4 == 2, prv, prv2)))


def _hgrn_scan_kernel(q_ref, f_ref, v_ref, g_ref, a_ref, fm_ref, vm_ref, gn_ref, y_ref, s_ref,
                      st_scr, mask_scr, *, n_heads, dk, dv, chunk, n_meta, n_chunks, n_levels):
    bi = pl.program_id(0)
    c = pl.program_id(1)
    row = lax.broadcasted_iota(jnp.int32, (chunk, chunk), 0)
    col = lax.broadcasted_iota(jnp.int32, (chunk, chunk), 1)

    @pl.when((bi == 0) & (c == 0))
    def _():
        mask_scr[0] = jnp.where(row == col, 1.0, 0.0)
        for l in range(1, n_levels + 1):
            same_block = (row >> l) == (col >> l)
            pair = (((row >> (l - 1)) & 1) == 1) & (((col >> (l - 1)) & 1) == 0)
            mask_scr[l] = jnp.where(same_block & pair, 1.0, 0.0)

    @pl.when(c == 0)
    def _():
        mr = lax.broadcasted_iota(jnp.int32, (n_meta, n_meta), 0)
        mc = lax.broadcasted_iota(jnp.int32, (n_meta, n_meta), 1)
        later = jnp.where(mc > mr, 1.0, 0.0).astype(BF16)

        def meta_body(h, carry):
            ks = pl.ds(pl.multiple_of(h * dk, dk), dk)
            vs = pl.ds(pl.multiple_of(h * dv, dv), dv)
            f = fm_ref[:, ks]
            tail = _split_dot(later, jnp.log(f))
            kd = ((1.0 - f) * jnp.exp(tail)).astype(BF16)
            st_scr[h] = _dot_tn(vm_ref[:, vs].astype(BF16), kd)
            return carry

        lax.fori_loop(0, n_heads, meta_body, 0)

    ltri = jnp.where(row >= col, 1.0, 0.0).astype(BF16)
    rows_k = lax.broadcasted_iota(jnp.int32, (chunk, dk), 0)

    def head_body(h, carry):
        ks = pl.ds(pl.multiple_of(h * dk, dk), dk)
        vs = pl.ds(pl.multiple_of(h * dv, dv), dv)
        f = f_ref[:, ks]
        k = 1.0 - f
        b = _split_dot(ltri, jnp.log(f))
        qb = q_ref[:, ks]
        q = qb.astype(F32)
        v = v_ref[:, vs]
        att = jnp.where(mask_scr[0] > 0.5, _dot_nt(qb, k.astype(BF16)), 0.0)
        for l in range(1, n_levels + 1):
            bnd = _level_boundary(b, l, rows_k)
            ql = (q * jnp.exp(b - bnd)).astype(BF16)
            kl = (k * jnp.exp(bnd - b)).astype(BF16)
            att = att + jnp.where(mask_scr[l] > 0.5, _dot_nt(ql, kl), 0.0)
        st = st_scr[h]
        o = _dot(att.astype(BF16), v) + _dot_nt((q * jnp.exp(b)).astype(BF16), st.astype(BF16))
        btot = b[chunk - 1:chunk, :]
        kd = (k * jnp.exp(btot - b)).astype(BF16)
        st_scr[h] = jnp.exp(btot) * st + _dot_tn(v, kd)
        ms = jnp.mean(o * o, axis=-1, keepdims=True)
        gates = _silu(g_ref[:, vs].astype(F32)) * _sigmoid(a_ref[:, vs].astype(F32))
        y_ref[:, vs] = (o * lax.rsqrt(ms + LN_EPS) * gn_ref[:, vs] * gates).astype(y_ref.dtype)
        return carry

    lax.fori_loop(0, n_heads, head_body, 0)

    @pl.when(c == n_chunks - 1)
    def _():
        for h in range(n_heads):
            s_ref[0, 0, h] = st_scr[h].T


def _hgrn_scan(qh, fg, rest, fg_s, rest_s, gn, *, batch, seq, n_heads, dk, dv, n_meta, meta_row0, depth):
    d = n_heads * dk
    chunk = _pick_tile(seq, HGRN_CHUNK, BF16_SUBLANES)
    assert chunk & (chunk - 1) == 0 and chunk >= 8
    n_levels = chunk.bit_length() - 1
    n_chunks = seq // chunk
    mblk = meta_row0 // n_meta
    row = lambda b, c: b * n_chunks + c
    return pl.pallas_call(
        functools.partial(_hgrn_scan_kernel, n_heads=n_heads, dk=dk, dv=dv, chunk=chunk,
                          n_meta=n_meta, n_chunks=n_chunks, n_levels=n_levels),
        grid=(batch, n_chunks),
        in_specs=[
            pl.BlockSpec((chunk, d), lambda b, c: (row(b, c), 0)),
            pl.BlockSpec((chunk, d), lambda b, c: (row(b, c), 0)),
            pl.BlockSpec((chunk, d), lambda b, c: (row(b, c), 0)),
            pl.BlockSpec((chunk, d), lambda b, c: (row(b, c), 1)),
            pl.BlockSpec((chunk, d), lambda b, c: (row(b, c), 3)),
            pl.BlockSpec((n_meta, d), lambda b, c: (mblk, 0)),
            pl.BlockSpec((n_meta, d), lambda b, c: (mblk, 0)),
            pl.BlockSpec((1, d), lambda b, c: (0, 0)),
        ],
        out_specs=[
            pl.BlockSpec((chunk, d), lambda b, c: (row(b, c), 0)),
            pl.BlockSpec((1, 1, n_heads, dk, dv), lambda b, c: (0, b, 0, 0, 0)),
        ],
        out_shape=[
            jax.ShapeDtypeStruct((batch * seq, d), BF16),
            jax.ShapeDtypeStruct((depth, batch, n_heads, dk, dv), F32),
        ],
        scratch_shapes=[
            pltpu.VMEM((n_heads, dv, dk), F32),
            pltpu.VMEM((n_levels + 1, chunk, chunk), F32),
        ],
        compiler_params=_params("arbitrary", "arbitrary"),
        name="hgrn_scan",
    )(qh, fg, rest, rest, rest, fg_s, rest_s, gn)


def _column_bcast(r, n_rows, n_cols):
    return jnp.broadcast_to(r, (n_cols, n_rows)).T


def _decode_kernel(qk_ref, vg_ref, qh_ref, f_ref, rest_ref, gn_ref, sr_ref, sh_ref,
                   y_ref, sro_ref, sho_ref, *, d, h_ret, dk_r, dv_r, h_hg, dk_h, dv_h):
    qk = qk_ref[0]
    vg = vg_ref[0]
    qh = qh_ref[0]
    fg = f_ref[0]
    rest = rest_ref[0]
    gn = gn_ref[...]
    out_r = []
    for h in range(h_ret):
        ks = slice(h * dk_r, (h + 1) * dk_r)
        vs = slice(h * dv_r, (h + 1) * dv_r)
        q = qk[:, ks]
        k = qk[:, d + h * dk_r:d + (h + 1) * dk_r]
        v = vg[:, vs]
        s_new = math.exp(_ret_log_gamma(h)) * sr_ref[0, 0, h] + _column_bcast(k, dk_r, dv_r) * v
        sro_ref[0, 0, h] = s_new
        o = _dot(jnp.broadcast_to(q, (8, dk_r)).astype(BF16), s_new.astype(BF16))[0:1, :]
        mu = jnp.mean(o, axis=-1, keepdims=True)
        dlt = o - mu
        var = jnp.mean(dlt * dlt, axis=-1, keepdims=True)
        gates = _silu(vg[:, d + h * dv_r:d + (h + 1) * dv_r]) * _sigmoid(rest[:, 2 * d + h * dv_r:2 * d + (h + 1) * dv_r])
        out_r.append(dlt * lax.rsqrt(var + LN_EPS) * gates)
    out_h = []
    for h in range(h_hg):
        ks = slice(h * dk_h, (h + 1) * dk_h)
        vs = slice(h * dv_h, (h + 1) * dv_h)
        f = fg[:, ks]
        v = rest[:, vs]
        s_new = _column_bcast(f, dk_h, dv_h) * sh_ref[0, 0, h] + _column_bcast(1.0 - f, dk_h, dv_h) * v
        sho_ref[0, 0, h] = s_new
        o = _dot(jnp.broadcast_to(qh[:, ks], (8, dk_h)).astype(BF16), s_new.astype(BF16))[0:1, :]
        ms = jnp.mean(o * o, axis=-1, keepdims=True)
        gates = _silu(rest[:, d + h * dv_h:d + (h + 1) * dv_h]) * _sigmoid(rest[:, 3 * d + h * dv_h:3 * d + (h + 1) * dv_h])
        out_h.append(o * lax.rsqrt(ms + LN_EPS) * gn[:, vs] * gates)
    y_ref[0] = jnp.concatenate(out_r, axis=1) + jnp.concatenate(out_h, axis=1)


def _decode(qk_s, vg_s, qh_s, fg_s, rest_s, gn, state_ret, state_hgrn, *, n_dec):
    depth, _, h_ret, dk_r, dv_r = state_ret.shape
    _, _, h_hg, dk_h, dv_h = state_hgrn.shape
    d = h_ret * dk_r
    rows = qk_s.shape[0]
    as3d = lambda a: a.reshape(rows, 1, a.shape[1])
    row_spec = lambda w: pl.BlockSpec((1, 1, w), lambda b: (b, 0, 0))
    ret_spec = pl.BlockSpec((1, 1, h_ret, dk_r, dv_r), lambda b: (0, b, 0, 0, 0))
    hg_spec = pl.BlockSpec((1, 1, h_hg, dk_h, dv_h), lambda b: (0, b, 0, 0, 0))
    y, sr, sh = pl.pallas_call(
        functools.partial(_decode_kernel, d=d, h_ret=h_ret, dk_r=dk_r, dv_r=dv_r,
                          h_hg=h_hg, dk_h=dk_h, dv_h=dv_h),
        grid=(n_dec,),
        in_specs=[row_spec(2 * d), row_spec(2 * d), row_spec(d), row_spec(d), row_spec(4 * d),
                  pl.BlockSpec((1, d), lambda b: (0, 0)), ret_spec, hg_spec],
        out_specs=[row_spec(d), ret_spec, hg_spec],
        out_shape=[
            jax.ShapeDtypeStruct((n_dec, 1, d), F32),
            jax.ShapeDtypeStruct(state_ret.shape, F32),
            jax.ShapeDtypeStruct(state_hgrn.shape, F32),
        ],
        compiler_params=_params("arbitrary"),
        name="decode",
    )(as3d(qk_s), as3d(vg_s), as3d(qh_s), as3d(fg_s), as3d(rest_s), gn, state_ret, state_hgrn)
    return y.reshape(n_dec, d), sr, sh


def _wout_ln_kernel(*refs, alpha, n_y):
    y_refs = refs[:n_y]
    x_ref, w_ref, g_ref, b_ref, o_ref = refs[n_y:]
    y = y_refs[0][...].astype(F32)
    for r in y_refs[1:]:
        y = y + r[...].astype(F32)
    m = _dot(y.astype(BF16), w_ref[...])
    o_ref[...] = _layer_norm(alpha * x_ref[...] + m, g_ref[...], b_ref[...])


def _wout_ln(ys, x, w, ln_g, ln_b, *, alpha, name):
    rows, d = x.shape
    tm = _pick_tile(rows, 512, BF16_SUBLANES)
    tile = pl.BlockSpec((tm, d), lambda i: (i, 0))
    vec = pl.BlockSpec((1, d), lambda i: (0, 0))
    return pl.pallas_call(
        functools.partial(_wout_ln_kernel, alpha=alpha, n_y=len(ys)),
        grid=(rows // tm,),
        in_specs=[tile] * len(ys) + [tile, pl.BlockSpec((d, d), lambda i: (0, 0)), vec, vec],
        out_specs=tile,
        out_shape=jax.ShapeDtypeStruct((rows, d), F32),
        compiler_params=_params("arbitrary"),
        name=name,
    )(*ys, x, w, ln_g, ln_b)


def _rotary_tables(pos, dk):
    inv = ROPE_BASE ** (-jnp.arange(0, dk, 2, dtype=F32) / dk)
    ang = pos.astype(F32)[:, None] * inv[None, :]
    cos, sin = jnp.cos(ang), jnp.sin(ang)
    cos_full = jnp.stack([cos, cos], axis=-1).reshape(pos.shape[0], dk)
    sin_signed = jnp.stack([-sin, sin], axis=-1).reshape(pos.shape[0], dk)
    return cos_full, sin_signed


def _ffn_weights(wg, wu, wd):
    f = wg.shape[1]
    fpad = -(-f // 512) * 512
    pad_c = ((0, 0), (0, fpad - f))
    return (jnp.pad(wg.astype(BF16), pad_c), jnp.pad(wu.astype(BF16), pad_c),
            jnp.pad(wd.astype(BF16), ((0, fpad - f), (0, 0))))


def kernel(x_prompt, x_sample, state_ret, state_hgrn, meta_tokens, ln1_g, ln1_b, ffn1_w_gate, ffn1_w_up, ffn1_w_down, w_in, hgrn_lb_logits, hgrn_norm_g, w_out, ln2_g, ln2_b, ffn2_w_gate, ffn2_w_up, ffn2_w_down, ln3_g, ln3_b):
    batch, seq, d = x_prompt.shape
    n_dec, dec_seq, _ = x_sample.shape
    depth, _, h_ret, dk_r, dv_r = state_ret.shape
    _, _, h_hg, dk_h, dv_h = state_hgrn.shape
    n_meta = meta_tokens.shape[0]
    assert depth == 1 and dec_seq == 1
    alpha = (2.0 * depth) ** 0.25
    layer = 0

    ffn1 = _ffn_weights(ffn1_w_gate[layer], ffn1_w_up[layer], ffn1_w_down[layer])
    ffn2 = _ffn_weights(ffn2_w_gate[layer], ffn2_w_up[layer], ffn2_w_down[layer])
    w_in_b = w_in[layer].astype(BF16)
    w_out_b = w_out[layer].astype(BF16)
    vec = lambda a: a[layer].reshape(1, d)
    lb = jnp.cumsum(jax.nn.softmax(hgrn_lb_logits.astype(F32), axis=0), axis=0)[layer].reshape(1, d)
    gn = vec(hgrn_norm_g)

    xp = x_prompt.reshape(batch * seq, d)
    xs = jnp.concatenate([x_sample.reshape(n_dec, d), meta_tokens.astype(x_prompt.dtype)], axis=0)
    n_small = xs.shape[0]
    cos_p, sin_p = _rotary_tables(n_meta + jnp.arange(seq, dtype=jnp.int32), dk_r)
    pos_s = jnp.concatenate([jnp.full((n_dec,), PAST_LEN, jnp.int32), jnp.arange(n_meta, dtype=jnp.int32)])
    cos_s, sin_s = _rotary_tables(pos_s, dk_r)

    x1p, x1pb = _ffn_ln(xp, *ffn1, vec(ln1_g), vec(ln1_b), alpha=alpha, emit_bf16=True, name="ffn1_prompt")
    x1s, x1sb = _ffn_ln(xs, *ffn1, vec(ln1_g), vec(ln1_b), alpha=alpha, emit_bf16=True, name="ffn1_small")

    qk_p, vg_p, qh_p, fg_p, rest_p = _project_all(x1pb, w_in_b, d, cos_p, sin_p, lb, dk_r, BF16, "prompt")
    qk_s, vg_s, qh_s, fg_s, rest_s = _project_all(x1sb, w_in_b, d, cos_s, sin_s, lb, dk_r, F32, "small")

    yr_p, state_ret_prompt = _ret_scan(qk_p, vg_p, rest_p, qk_s, vg_s, batch=batch, seq=seq, n_heads=h_ret,
                                       dk=dk_r, dv=dv_r, n_meta=n_meta, meta_row0=n_dec, depth=depth)
    yh_p, state_hgrn_prompt = _hgrn_scan(qh_p, fg_p, rest_p, fg_s, rest_s, gn, batch=batch, seq=seq,
                                         n_heads=h_hg, dk=dk_h, dv=dv_h, n_meta=n_meta, meta_row0=n_dec,
                                         depth=depth)
    y_s, state_ret_sample, state_hgrn_sample = _decode(qk_s, vg_s, qh_s, fg_s, rest_s, gn, state_ret,
                                                       state_hgrn, n_dec=n_dec)

    x2p = _wout_ln([yr_p, yh_p], x1p, w_out_b, vec(ln2_g), vec(ln2_b), alpha=alpha, name="wout_prompt")
    x2s = _wout_ln([y_s], x1s[:n_dec], w_out_b, vec(ln2_g), vec(ln2_b), alpha=alpha, name="wout_small")

    y_prompt = _ffn_ln(x2p, *ffn2, vec(ln3_g), vec(ln3_b), alpha=alpha, emit_bf16=False, name="ffn2_prompt")
    y_sample = _ffn_ln(x2s, *ffn2, vec(ln3_g), vec(ln3_b), alpha=alpha, emit_bf16=False, name="ffn2_small")

    return (y_prompt.reshape(batch, seq, d), y_sample.reshape(n_dec, dec_seq, d), state_ret_prompt,
            state_ret_sample, state_hgrn_prompt, state_hgrn_sample)
```

```python
import functools
import math

import jax
import jax.numpy as jnp
from jax import lax
from jax.experimental import pallas as pl
from jax.experimental.pallas import tpu as pltpu

F32 = jnp.float32
BF16 = jnp.bfloat16

PAST_LEN = 16384
LN_EPS = 1e-5
ROPE_BASE = 10000.0

V7X_VMEM_BYTES = 64 * 1024 * 1024
VMEM_LIMIT_BYTES = V7X_VMEM_BYTES - 8 * 1024 * 1024
LANES = 128
BF16_SUBLANES = 16

RET_CHUNK = 256
HGRN_CHUNK = 128
HGRN_HEAD_UNROLL = 8


def _params(*sem):
    return pltpu.CompilerParams(dimension_semantics=sem, vmem_limit_bytes=VMEM_LIMIT_BYTES)


def _dot(a, b):
    return jnp.dot(a, b, preferred_element_type=F32)


def _dot_nt(a, b):
    return lax.dot_general(a, b, (((1,), (1,)), ((), ())), preferred_element_type=F32)


def _dot_tn(a, b):
    return lax.dot_general(a, b, (((0,), (0,)), ((), ())), preferred_element_type=F32)


def _sigmoid(x):
    return jax.nn.sigmoid(x)


def _silu(x):
    return x * jax.nn.sigmoid(x)


def _layer_norm(y, g, b):
    mu = jnp.mean(y, axis=-1, keepdims=True)
    d = y - mu
    var = jnp.mean(d * d, axis=-1, keepdims=True)
    return d * lax.rsqrt(var + LN_EPS) * g + b


def _pick_tile(n, target, quantum):
    if n <= target:
        return n
    t = (target // quantum) * quantum
    while t > quantum and n % t:
        t -= quantum
    assert n % t == 0, (n, target, quantum)
    return t


def _ffn_ln_kernel(x_ref, wg_ref, wu_ref, wd_ref, g_ref, b_ref, *rest, alpha, n_ff, emit_bf16):
    if emit_bf16:
        o_ref, ob_ref, xb_scr = rest
    else:
        (o_ref, xb_scr), ob_ref = rest, None
    j = pl.program_id(1)

    @pl.when(j == 0)
    def _():
        xb_scr[...] = x_ref[...].astype(BF16)
        o_ref[...] = jnp.zeros_like(o_ref)

    xb = xb_scr[...]
    gate = _dot(xb, wg_ref[...])
    up = _dot(xb, wu_ref[...])
    h = (_silu(gate) * up).astype(BF16)
    o_ref[...] += _dot(h, wd_ref[...])

    @pl.when(j == n_ff - 1)
    def _():
        y = _layer_norm(alpha * x_ref[...] + 0.5 * o_ref[...], g_ref[...], b_ref[...])
        o_ref[...] = y
        if emit_bf16:
            ob_ref[...] = y.astype(BF16)


def _ffn_ln(x, wg, wu, wd, ln_g, ln_b, *, alpha, emit_bf16, name):
    rows, d = x.shape
    fpad = wg.shape[1]
    tm = _pick_tile(rows, 512, BF16_SUBLANES)
    tf = _pick_tile(fpad, 512, LANES)
    n_ff = fpad // tf
    out_shape = [jax.ShapeDtypeStruct((rows, d), F32)]
    out_specs = [pl.BlockSpec((tm, d), lambda i, j: (i, 0))]
    if emit_bf16:
        out_shape.append(jax.ShapeDtypeStruct((rows, d), BF16))
        out_specs.append(pl.BlockSpec((tm, d), lambda i, j: (i, 0)))
    res = pl.pallas_call(
        functools.partial(_ffn_ln_kernel, alpha=alpha, n_ff=n_ff, emit_bf16=emit_bf16),
        grid=(rows // tm, n_ff),
        in_specs=[
            pl.BlockSpec((tm, d), lambda i, j: (i, 0)),
            pl.BlockSpec((d, tf), lambda i, j: (0, j)),
            pl.BlockSpec((d, tf), lambda i, j: (0, j)),
            pl.BlockSpec((tf, d), lambda i, j: (j, 0)),
            pl.BlockSpec((1, d), lambda i, j: (0, 0)),
            pl.BlockSpec((1, d), lambda i, j: (0, 0)),
        ],
        out_specs=out_specs,
        out_shape=out_shape,
        scratch_shapes=[pltpu.VMEM((tm, d), BF16)],
        compiler_params=_params("arbitrary", "arbitrary"),
        name=name,
    )(x, wg, wu, wd, ln_g, ln_b)
    return res if emit_bf16 else res[0]


def _apply_epilogue(kind, acc):
    if kind == "silu":
        return _silu(acc)
    if kind == "sigmoid":
        return _sigmoid(acc)
    assert kind == "none", kind
    return acc


def _proj_kernel(x_ref, w_ref, *rest, epilogues, tiles_per_group, dk, k_scale):
    o_ref = rest[-1]
    acc = _dot(x_ref[...], w_ref[...])
    group = pl.program_id(0) // tiles_per_group
    if epilogues == ("fgate",):
        lb = rest[0][...]
        o_ref[...] = (lb + (1.0 - lb) * _sigmoid(acc)).astype(o_ref.dtype)
    elif epilogues == ("rot_q", "rot_k"):
        cos = rest[0][...]
        sin = rest[1][...]
        scale = jnp.where(group == 0, 1.0, k_scale).astype(F32)
        even = (lax.broadcasted_iota(jnp.int32, cos.shape, 1) & 1) == 0
        for hh in range(acc.shape[1] // dk):
            a = acc[:, hh * dk:(hh + 1) * dk]
            swapped = jnp.where(even, pltpu.roll(a, dk - 1, 1), pltpu.roll(a, 1, 1))
            o_ref[:, hh * dk:(hh + 1) * dk] = ((a * cos + swapped * sin) * scale).astype(o_ref.dtype)
    else:
        kinds = sorted(set(epilogues))
        for kind in kinds:
            def store(kind=kind):
                o_ref[...] = _apply_epilogue(kind, acc).astype(o_ref.dtype)
            if len(kinds) == 1:
                store()
            else:
                hit = functools.reduce(jnp.logical_or, [group == g for g, e in enumerate(epilogues) if e == kind])
                pl.when(hit)(store)


def _proj(xb, w, col0, gw, *, epilogues, out_dtype, name, extras=(), dk=0, k_scale=1.0):
    rows, d = xb.shape
    ncols = gw * len(epilogues)
    rotary = epilogues[0].startswith("rot")
    tm = _pick_tile(extras[0].shape[0] if rotary else rows, 1024, BF16_SUBLANES)
    tn = _pick_tile(gw, 1024, max(LANES, dk))
    assert col0 % tn == 0 and rows % tm == 0
    in_specs = [
        pl.BlockSpec((tm, d), lambda n, m: (m, 0)),
        pl.BlockSpec((d, tn), lambda n, m: (0, col0 // tn + n)),
    ]
    if rotary:
        tab_blocks = extras[0].shape[0] // tm
        in_specs += [pl.BlockSpec((tm, dk), lambda n, m: (m % tab_blocks, 0))] * 2
    elif epilogues == ("fgate",):
        in_specs += [pl.BlockSpec((1, tn), lambda n, m: (0, n))]
    return pl.pallas_call(
        functools.partial(_proj_kernel, epilogues=epilogues, tiles_per_group=gw // tn, dk=dk, k_scale=k_scale),
        grid=(ncols // tn, rows // tm),
        in_specs=in_specs,
        out_specs=pl.BlockSpec((tm, tn), lambda n, m: (m, n)),
        out_shape=jax.ShapeDtypeStruct((rows, ncols), out_dtype),
        compiler_params=_params("arbitrary", "arbitrary"),
        name=name,
    )(xb, w, *extras)


def _project_all(xb, w_in, d, cos, sin, lb, dk_ret, out_dtype, tag):
    qk = _proj(xb, w_in, 0, d, epilogues=("rot_q", "rot_k"), out_dtype=out_dtype, extras=(cos, sin),
               dk=dk_ret, k_scale=dk_ret ** -0.5, name=f"proj_qk_{tag}")
    vgq = _proj(xb, w_in, 2 * d, d, epilogues=("none", "silu", "silu"), out_dtype=out_dtype,
                name=f"proj_vgq_{tag}")
    fg = _proj(xb, w_in, 5 * d, d, epilogues=("fgate",), out_dtype=F32, extras=(lb,), name=f"proj_f_{tag}")
    rest = _proj(xb, w_in, 6 * d, d, epilogues=("none", "silu", "sigmoid", "sigmoid"), out_dtype=out_dtype,
                 name=f"proj_rest_{tag}")
    return qk, vgq, fg, rest


def _ret_log2_gamma(h):
    return math.log2(1.0 - 2.0 ** (-5.0 - h))


def _ret_scan_kernel(q_ref, k_ref, v_ref, g_ref, a_ref, km_ref, vm_ref, y_ref, s_ref, s_scr, *,
                     n_heads, dk, dv, chunk, n_meta, n_chunks):
    c = pl.program_id(1)

    @pl.when(c == 0)
    def _():
        mpos = lax.broadcasted_iota(jnp.int32, (n_meta, dk), 0).astype(F32)
        for h in range(n_heads):
            lg = _ret_log2_gamma(h)
            kd = (km_ref[:, h * dk:(h + 1) * dk].astype(F32) * jnp.exp2(lg * (n_meta - 1.0 - mpos))).astype(BF16)
            s_scr[h] = _dot_tn(kd, vm_ref[:, h * dv:(h + 1) * dv].astype(BF16))

    rel = (lax.broadcasted_iota(jnp.int32, (chunk, chunk), 0)
           - lax.broadcasted_iota(jnp.int32, (chunk, chunk), 1)).astype(F32)
    pos_k = lax.broadcasted_iota(jnp.int32, (chunk, dk), 0).astype(F32)
    pos_v = lax.broadcasted_iota(jnp.int32, (chunk, dv), 0).astype(F32)
    for h in range(n_heads):
        lg = _ret_log2_gamma(h)
        ks = slice(h * dk, (h + 1) * dk)
        vs = slice(h * dv, (h + 1) * dv)
        q = q_ref[:, ks]
        k = k_ref[:, ks]
        v = v_ref[:, vs]
        s = s_scr[h]
        decay = jnp.where(rel >= 0, jnp.exp2(lg * jnp.maximum(rel, 0.0)), 0.0)
        inner = (_dot_nt(q, k) * decay).astype(BF16)
        o = _dot(inner, v) + _dot(q, s.astype(BF16)) * jnp.exp2(lg * (pos_v + 1.0))
        kd = (k.astype(F32) * jnp.exp2(lg * (chunk - 1.0 - pos_k))).astype(BF16)
        s_scr[h] = 2.0 ** (lg * chunk) * s + _dot_tn(kd, v)
        mu = jnp.mean(o, axis=-1, keepdims=True)
        dlt = o - mu
        var = jnp.mean(dlt * dlt, axis=-1, keepdims=True)
        gates = g_ref[:, vs].astype(F32) * a_ref[:, vs].astype(F32)
        y_ref[:, vs] = (dlt * lax.rsqrt(var + LN_EPS) * gates).astype(y_ref.dtype)

    @pl.when(c == n_chunks - 1)
    def _():
        s_ref[0, 0] = s_scr[...]


def _ret_scan(qk, vgq, rest, qk_s, vgq_s, *, batch, seq, n_heads, dk, dv, n_meta, meta_row0, depth):
    d = n_heads * dk
    chunk = _pick_tile(seq, RET_CHUNK, BF16_SUBLANES)
    n_chunks = seq // chunk
    assert meta_row0 % n_meta == 0
    mblk = meta_row0 // n_meta
    row = lambda b, c: b * n_chunks + c
    return pl.pallas_call(
        functools.partial(_ret_scan_kernel, n_heads=n_heads, dk=dk, dv=dv, chunk=chunk,
                          n_meta=n_meta, n_chunks=n_chunks),
        grid=(batch, n_chunks),
        in_specs=[
            pl.BlockSpec((chunk, d), lambda b, c: (row(b, c), 0)),
            pl.BlockSpec((chunk, d), lambda b, c: (row(b, c), 1)),
            pl.BlockSpec((chunk, d), lambda b, c: (row(b, c), 0)),
            pl.BlockSpec((chunk, d), lambda b, c: (row(b, c), 1)),
            pl.BlockSpec((chunk, d), lambda b, c: (row(b, c), 2)),
            pl.BlockSpec((n_meta, d), lambda b, c: (mblk, 1)),
            pl.BlockSpec((n_meta, d), lambda b, c: (mblk, 0)),
        ],
        out_specs=[
            pl.BlockSpec((chunk, d), lambda b, c: (row(b, c), 0)),
            pl.BlockSpec((1, 1, n_heads, dk, dv), lambda b, c: (0, b, 0, 0, 0)),
        ],
        out_shape=[
            jax.ShapeDtypeStruct((batch * seq, d), BF16),
            jax.ShapeDtypeStruct((depth, batch, n_heads, dk, dv), F32),
        ],
        scratch_shapes=[pltpu.VMEM((n_heads, dk, dv), F32)],
        compiler_params=_params("arbitrary", "arbitrary"),
        name="ret_scan",
    )(qk, qk, vgq, vgq, rest, qk_s, vgq_s)


def _split_dot(m01, x):
    hi = x.astype(BF16)
    lo = (x - hi.astype(F32)).astype(BF16)
    return _dot(m01, hi) + _dot(m01, lo)


def _level_boundary(b, level, rows):
    n, width = b.shape
    m = 1 << level
    half = m >> 1
    if m >= 8:
        pieces = [jnp.broadcast_to(b[p * m + half - 1:p * m + half, :], (m, width)) for p in range(n // m)]
        return pieces[0] if len(pieces) == 1 else jnp.concatenate(pieces, axis=0)
    if m == 2:
        return jnp.where((rows & 1) == 1, pltpu.roll(b, 1, 0), b)
    r4 = rows & 3
    nxt = pltpu.roll(b, n - 1, 0)
    prv = pltpu.roll(b, 1, 0)
    prv2 = pltpu.roll(b, 2, 0)
    return jnp.where(r4 == 0, nxt, jnp.where(r4 == 1, b, jnp.where(r4 == 2, prv, prv2)))


def _hgrn_scan_kernel(q_ref, f_ref, v_ref, g_ref, a_ref, fm_ref, vm_ref, gn_ref, y_ref, s_ref,
                      st_scr, mask_scr, sign_scr, *, n_heads, dk, dv, chunk, n_meta, n_chunks, n_levels):
    bi = pl.program_id(0)
    c = pl.program_id(1)
    row = lax.broadcasted_iota(jnp.int32, (chunk, chunk), 0)
    col = lax.broadcasted_iota(jnp.int32, (chunk, chunk), 1)
    rows_k = lax.broadcasted_iota(jnp.int32, (chunk, dk), 0)

    @pl.when((bi == 0) & (c == 0))
    def _():
        mask_scr[0] = jnp.where(row == col, 1.0, 0.0)
        for l in range(1, n_levels + 1):
            same_block = (row >> l) == (col >> l)
            pair = (((row >> (l - 1)) & 1) == 1) & (((col >> (l - 1)) & 1) == 0)
            mask_scr[l] = jnp.where(same_block & pair, 1.0, 0.0)
            sign_scr[l - 1] = jnp.where(((rows_k >> (l - 1)) & 1) == 1, 1.0, -1.0)

    @pl.when(c == 0)
    def _():
        mr = lax.broadcasted_iota(jnp.int32, (n_meta, n_meta), 0)
        mc = lax.broadcasted_iota(jnp.int32, (n_meta, n_meta), 1)
        later = jnp.where(mc > mr, 1.0, 0.0).astype(BF16)

        def meta_body(h, carry):
            ks = pl.ds(pl.multiple_of(h * dk, dk), dk)
            vs = pl.ds(pl.multiple_of(h * dv, dv), dv)
            f = fm_ref[:, ks]
            tail = _split_dot(later, jnp.log2(f))
            kd = ((1.0 - f) * jnp.exp2(tail)).astype(BF16)
            st_scr[h] = _dot_tn(vm_ref[:, vs].astype(BF16), kd)
            return carry

        lax.fori_loop(0, n_heads, meta_body, 0)

    ltri = jnp.where(row >= col, 1.0, 0.0).astype(BF16)

    def head_body(h, carry):
        ks = pl.ds(pl.multiple_of(h * dk, dk), dk)
        vs = pl.ds(pl.multiple_of(h * dv, dv), dv)
        f = f_ref[:, ks]
        k = 1.0 - f
        b = _split_dot(ltri, jnp.log2(f))
        qb = q_ref[:, ks]
        q = qb.astype(F32)
        v = v_ref[:, vs]
        att = mask_scr[0] * _dot_nt(qb, k.astype(BF16))
        for l in range(1, n_levels + 1):
            e = jnp.exp2((b - _level_boundary(b, l, rows_k)) * sign_scr[l - 1])
            att = att + mask_scr[l] * _dot_nt((q * e).astype(BF16), (k * e).astype(BF16))
        st = st_scr[h]
        o = _dot(att.astype(BF16), v) + _dot_nt((q * jnp.exp2(b)).astype(BF16), st.astype(BF16))
        btot = b[chunk - 1:chunk, :]
        kd = (k * jnp.exp2(btot - b)).astype(BF16)
        st_scr[h] = jnp.exp2(btot) * st + _dot_tn(v, kd)
        ms = jnp.mean(o * o, axis=-1, keepdims=True)
        gates = g_ref[:, vs].astype(F32) * a_ref[:, vs].astype(F32)
        y_ref[:, vs] = (o * lax.rsqrt(ms + LN_EPS) * gn_ref[:, vs] * gates).astype(y_ref.dtype)
        return carry

    lax.fori_loop(0, n_heads, head_body, 0, unroll=HGRN_HEAD_UNROLL)

    @pl.when(c == n_chunks - 1)
    def _():
        for h in range(n_heads):
            s_ref[0, 0, h] = st_scr[h].T


def _hgrn_scan(vgq, fg, rest, fg_s, rest_s, gn, *, batch, seq, n_heads, dk, dv, n_meta, meta_row0, depth):
    d = n_heads * dk
    chunk = _pick_tile(seq, HGRN_CHUNK, BF16_SUBLANES)
    assert chunk & (chunk - 1) == 0 and chunk >= 8
    n_levels = chunk.bit_length() - 1
    n_chunks = seq // chunk
    mblk = meta_row0 // n_meta
    row = lambda b, c: b * n_chunks + c
    return pl.pallas_call(
        functools.partial(_hgrn_scan_kernel, n_heads=n_heads, dk=dk, dv=dv, chunk=chunk,
                          n_meta=n_meta, n_chunks=n_chunks, n_levels=n_levels),
        grid=(batch, n_chunks),
        in_specs=[
            pl.BlockSpec((chunk, d), lambda b, c: (row(b, c), 2)),
            pl.BlockSpec((chunk, d), lambda b, c: (row(b, c), 0)),
            pl.BlockSpec((chunk, d), lambda b, c: (row(b, c), 0)),
            pl.BlockSpec((chunk, d), lambda b, c: (row(b, c), 1)),
            pl.BlockSpec((chunk, d), lambda b, c: (row(b, c), 3)),
            pl.BlockSpec((n_meta, d), lambda b, c: (mblk, 0)),
            pl.BlockSpec((n_meta, d), lambda b, c: (mblk, 0)),
            pl.BlockSpec((1, d), lambda b, c: (0, 0)),
        ],
        out_specs=[
            pl.BlockSpec((chunk, d), lambda b, c: (row(b, c), 0)),
            pl.BlockSpec((1, 1, n_heads, dk, dv), lambda b, c: (0, b, 0, 0, 0)),
        ],
        out_shape=[
            jax.ShapeDtypeStruct((batch * seq, d), BF16),
            jax.ShapeDtypeStruct((depth, batch, n_heads, dk, dv), F32),
        ],
        scratch_shapes=[
            pltpu.VMEM((n_heads, dv, dk), F32),
            pltpu.VMEM((n_levels + 1, chunk, chunk), F32),
            pltpu.VMEM((n_levels, chunk, dk), F32),
        ],
        compiler_params=_params("arbitrary", "arbitrary"),
        name="hgrn_scan",
    )(vgq, fg, rest, rest, rest, fg_s, rest_s, gn)


def _column_bcast(r, n_rows, n_cols):
    return jnp.broadcast_to(r, (n_cols, n_rows)).T


def _decode_kernel(qk_ref, vgq_ref, f_ref, rest_ref, gn_ref, sr_ref, sh_ref,
                   y_ref, sro_ref, sho_ref, *, d, h_ret, dk_r, dv_r, h_hg, dk_h, dv_h):
    qk = qk_ref[0]
    vgq = vgq_ref[0]
    fg = f_ref[0]
    rest = rest_ref[0]
    gn = gn_ref[...]
    out_r = []
    for h in range(h_ret):
        q = qk[:, h * dk_r:(h + 1) * dk_r]
        k = qk[:, d + h * dk_r:d + (h + 1) * dk_r]
        v = vgq[:, h * dv_r:(h + 1) * dv_r]
        s_new = 2.0 ** _ret_log2_gamma(h) * sr_ref[0, 0, h] + _column_bcast(k, dk_r, dv_r) * v
        sro_ref[0, 0, h] = s_new
        o = _dot(jnp.broadcast_to(q, (8, dk_r)).astype(BF16), s_new.astype(BF16))[0:1, :]
        mu = jnp.mean(o, axis=-1, keepdims=True)
        dlt = o - mu
        var = jnp.mean(dlt * dlt, axis=-1, keepdims=True)
        gates = vgq[:, d + h * dv_r:d + (h + 1) * dv_r] * rest[:, 2 * d + h * dv_r:2 * d + (h + 1) * dv_r]
        out_r.append(dlt * lax.rsqrt(var + LN_EPS) * gates)
    out_h = []
    for h in range(h_hg):
        ks = slice(h * dk_h, (h + 1) * dk_h)
        vs = slice(h * dv_h, (h + 1) * dv_h)
        f = fg[:, ks]
        q = vgq[:, 2 * d + h * dk_h:2 * d + (h + 1) * dk_h]
        v = rest[:, vs]
        s_new = _column_bcast(f, dk_h, dv_h) * sh_ref[0, 0, h] + _column_bcast(1.0 - f, dk_h, dv_h) * v
        sho_ref[0, 0, h] = s_new
        o = _dot(jnp.broadcast_to(q, (8, dk_h)).astype(BF16), s_new.astype(BF16))[0:1, :]
        ms = jnp.mean(o * o, axis=-1, keepdims=True)
        gates = rest[:, d + h * dv_h:d + (h + 1) * dv_h] * rest[:, 3 * d + h * dv_h:3 * d + (h + 1) * dv_h]
        out_h.append(o * lax.rsqrt(ms + LN_EPS) * gn[:, vs] * gates)
    y_ref[0] = jnp.concatenate(out_r, axis=1) + jnp.concatenate(out_h, axis=1)


def _decode(qk_s, vgq_s, fg_s, rest_s, gn, state_ret, state_hgrn, *, n_dec):
    depth, _, h_ret, dk_r, dv_r = state_ret.shape
    _, _, h_hg, dk_h, dv_h = state_hgrn.shape
    d = h_ret * dk_r
    rows = qk_s.shape[0]
    as3d = lambda a: a.reshape(rows, 1, a.shape[1])
    row_spec = lambda w: pl.BlockSpec((1, 1, w), lambda b: (b, 0, 0))
    ret_spec = pl.BlockSpec((1, 1, h_ret, dk_r, dv_r), lambda b: (0, b, 0, 0, 0))
    hg_spec = pl.BlockSpec((1, 1, h_hg, dk_h, dv_h), lambda b: (0, b, 0, 0, 0))
    y, sr, sh = pl.pallas_call(
        functools.partial(_decode_kernel, d=d, h_ret=h_ret, dk_r=dk_r, dv_r=dv_r,
                          h_hg=h_hg, dk_h=dk_h, dv_h=dv_h),
        grid=(n_dec,),
        in_specs=[row_spec(2 * d), row_spec(3 * d), row_spec(d), row_spec(4 * d),
                  pl.BlockSpec((1, d), lambda b: (0, 0)), ret_spec, hg_spec],
        out_specs=[row_spec(d), ret_spec, hg_spec],
        out_shape=[
            jax.ShapeDtypeStruct((n_dec, 1, d), F32),
            jax.ShapeDtypeStruct(state_ret.shape, F32),
            jax.ShapeDtypeStruct(state_hgrn.shape, F32),
        ],
        compiler_params=_params("arbitrary"),
        name="decode",
    )(as3d(qk_s), as3d(vgq_s), as3d(fg_s), as3d(rest_s), gn, state_ret, state_hgrn)
    return y.reshape(n_dec, d), sr, sh


def _wout_ln_kernel(*refs, alpha, n_y):
    y_refs = refs[:n_y]
    x_ref, w_ref, g_ref, b_ref, o_ref = refs[n_y:]
    y = y_refs[0][...].astype(F32)
    for r in y_refs[1:]:
        y = y + r[...].astype(F32)
    m = _dot(y.astype(BF16), w_ref[...])
    o_ref[...] = _layer_norm(alpha * x_ref[...] + m, g_ref[...], b_ref[...])


def _wout_ln(ys, x, w, ln_g, ln_b, *, alpha, name):
    rows, d = x.shape
    tm = _pick_tile(rows, 512, BF16_SUBLANES)
    tile = pl.BlockSpec((tm, d), lambda i: (i, 0))
    vec = pl.BlockSpec((1, d), lambda i: (0, 0))
    return pl.pallas_call(
        functools.partial(_wout_ln_kernel, alpha=alpha, n_y=len(ys)),
        grid=(rows // tm,),
        in_specs=[tile] * len(ys) + [tile, pl.BlockSpec((d, d), lambda i: (0, 0)), vec, vec],
        out_specs=tile,
        out_shape=jax.ShapeDtypeStruct((rows, d), F32),
        compiler_params=_params("arbitrary"),
        name=name,
    )(*ys, x, w, ln_g, ln_b)


def _rotary_tables(pos, dk):
    inv = ROPE_BASE ** (-jnp.arange(0, dk, 2, dtype=F32) / dk)
    ang = pos.astype(F32)[:, None] * inv[None, :]
    cos, sin = jnp.cos(ang), jnp.sin(ang)
    cos_full = jnp.stack([cos, cos], axis=-1).reshape(pos.shape[0], dk)
    sin_signed = jnp.stack([-sin, sin], axis=-1).reshape(pos.shape[0], dk)
    return cos_full, sin_signed


def _ffn_weights(wg, wu, wd):
    f = wg.shape[1]
    fpad = -(-f // 512) * 512
    pad_c = ((0, 0), (0, fpad - f))
    return (jnp.pad(wg.astype(BF16), pad_c), jnp.pad(wu.astype(BF16), pad_c),
            jnp.pad(wd.astype(BF16), ((0, fpad - f), (0, 0))))


def kernel(x_prompt, x_sample, state_ret, state_hgrn, meta_tokens, ln1_g, ln1_b, ffn1_w_gate, ffn1_w_up, ffn1_w_down, w_in, hgrn_lb_logits, hgrn_norm_g, w_out, ln2_g, ln2_b, ffn2_w_gate, ffn2_w_up, ffn2_w_down, ln3_g, ln3_b):
    batch, seq, d = x_prompt.shape
    n_dec, dec_seq, _ = x_sample.shape
    depth, _, h_ret, dk_r, dv_r = state_ret.shape
    _, _, h_hg, dk_h, dv_h = state_hgrn.shape
    n_meta = meta_tokens.shape[0]
    assert depth == 1 and dec_seq == 1
    alpha = (2.0 * depth) ** 0.25
    layer = 0

    ffn1 = _ffn_weights(ffn1_w_gate[layer], ffn1_w_up[layer], ffn1_w_down[layer])
    ffn2 = _ffn_weights(ffn2_w_gate[layer], ffn2_w_up[layer], ffn2_w_down[layer])
    w_in_b = w_in[layer].astype(BF16)
    w_out_b = w_out[layer].astype(BF16)
    vec = lambda a: a[layer].reshape(1, d)
    lb = jnp.cumsum(jax.nn.softmax(hgrn_lb_logits.astype(F32), axis=0), axis=0)[layer].reshape(1, d)
    gn = vec(hgrn_norm_g)

    xp = x_prompt.reshape(batch * seq, d)
    xs = jnp.concatenate([x_sample.reshape(n_dec, d), meta_tokens.astype(x_prompt.dtype)], axis=0)
    cos_p, sin_p = _rotary_tables(n_meta + jnp.arange(seq, dtype=jnp.int32), dk_r)
    pos_s = jnp.concatenate([jnp.full((n_dec,), PAST_LEN, jnp.int32), jnp.arange(n_meta, dtype=jnp.int32)])
    cos_s, sin_s = _rotary_tables(pos_s, dk_r)

    x1p, x1pb = _ffn_ln(xp, *ffn1, vec(ln1_g), vec(ln1_b), alpha=alpha, emit_bf16=True, name="ffn1_prompt")
    x1s, x1sb = _ffn_ln(xs, *ffn1, vec(ln1_g), vec(ln1_b), alpha=alpha, emit_bf16=True, name="ffn1_small")

    qk_p, vgq_p, fg_p, rest_p = _project_all(x1pb, w_in_b, d, cos_p, sin_p, lb, dk_r, BF16, "prompt")
    qk_s, vgq_s, fg_s, rest_s = _project_all(x1sb, w_in_b, d, cos_s, sin_s, lb, dk_r, F32, "small")

    yr_p, state_ret_prompt = _ret_scan(qk_p, vgq_p, rest_p, qk_s, vgq_s, batch=batch, seq=seq, n_heads=h_ret,
                                       dk=dk_r, dv=dv_r, n_meta=n_meta, meta_row0=n_dec, depth=depth)
    yh_p, state_hgrn_prompt = _hgrn_scan(vgq_p, fg_p, rest_p, fg_s, rest_s, gn, batch=batch, seq=seq,
                                         n_heads=h_hg, dk=dk_h, dv=dv_h, n_meta=n_meta, meta_row0=n_dec,
                                         depth=depth)
    y_s, state_ret_sample, state_hgrn_sample = _decode(qk_s, vgq_s, fg_s, rest_s, gn, state_ret, state_hgrn,
                                                       n_dec=n_dec)

    x2p = _wout_ln([yr_p, yh_p], x1p, w_out_b, vec(ln2_g), vec(ln2_b), alpha=alpha, name="wout_prompt")
    x2s = _wout_ln([y_s], x1s[:n_dec], w_out_b, vec(ln2_g), vec(ln2_b), alpha=alpha, name="wout_small")

    y_prompt = _ffn_ln(x2p, *ffn2, vec(ln3_g), vec(ln3_b), alpha=alpha, emit_bf16=False, name="ffn2_prompt")
    y_sample = _ffn_ln(x2s, *ffn2, vec(ln3_g), vec(ln3_b), alpha=alpha, emit_bf16=False, name="ffn2_small")

    return (y_prompt.reshape(batch, seq, d), y_sample.reshape(n_dec, dec_seq, d), state_ret_prompt,
            state_ret_sample, state_hgrn_prompt, state_hgrn_sample)
```

```python
import functools
import math

import jax
import jax.numpy as jnp
from jax import lax
from jax.experimental import pallas as pl
from jax.experimental.pallas import tpu as pltpu

F32 = jnp.float32
BF16 = jnp.bfloat16

PAST_LEN = 16384
LN_EPS = 1e-5
ROPE_BASE = 10000.0

V7X_VMEM_BYTES = 64 * 1024 * 1024
VMEM_LIMIT_BYTES = V7X_VMEM_BYTES - 8 * 1024 * 1024
LANES = 128
BF16_SUBLANES = 16

FFN_HIDDEN_TILE = 256
RET_CHUNK = 256
HGRN_CHUNK = 128
HGRN_HEAD_UNROLL = 8


def _params(*sem):
    return pltpu.CompilerParams(dimension_semantics=sem, vmem_limit_bytes=VMEM_LIMIT_BYTES)


def _dot(a, b):
    return jnp.dot(a, b, preferred_element_type=F32)


def _dot_nt(a, b):
    return lax.dot_general(a, b, (((1,), (1,)), ((), ())), preferred_element_type=F32)


def _dot_tn(a, b):
    return lax.dot_general(a, b, (((0,), (0,)), ((), ())), preferred_element_type=F32)


def _sigmoid(x):
    return 0.5 * jnp.tanh(0.5 * x) + 0.5


def _silu(x):
    return x * _sigmoid(x)


def _layer_norm(y, g, b):
    mu = jnp.mean(y, axis=-1, keepdims=True)
    d = y - mu
    var = jnp.mean(d * d, axis=-1, keepdims=True)
    return d * lax.rsqrt(var + LN_EPS) * g + b


def _pick_tile(n, target, quantum):
    if n <= target:
        return n
    t = (target // quantum) * quantum
    while t > quantum and n % t:
        t -= quantum
    assert n % t == 0, (n, target, quantum)
    return t


def _ffn_ln_kernel(x_ref, g_ref, b_ref, *rest, alpha, n_main, has_tail, emit_bf16):
    main_w, rest = rest[:3], rest[3:]
    tail_w, rest = (rest[:3], rest[3:]) if has_tail else (None, rest)
    if emit_bf16:
        o_ref, ob_ref, xb_scr = rest
    else:
        (o_ref, xb_scr), ob_ref = rest, None
    j = pl.program_id(1)
    n_steps = n_main + int(has_tail)

    @pl.when(j == 0)
    def _():
        xb_scr[...] = x_ref[...].astype(BF16)
        o_ref[...] = jnp.zeros_like(o_ref)

    def accumulate(wg_ref, wu_ref, wd_ref):
        xb = xb_scr[...]
        gate = _dot(xb, wg_ref[...].astype(BF16))
        up = _dot(xb, wu_ref[...].astype(BF16))
        h = (_silu(gate) * up).astype(BF16)
        o_ref[...] += _dot(h, wd_ref[...].astype(BF16))

    if has_tail:
        pl.when(j < n_main)(functools.partial(accumulate, *main_w))
        pl.when(j == n_main)(functools.partial(accumulate, *tail_w))
    else:
        accumulate(*main_w)

    @pl.when(j == n_steps - 1)
    def _():
        y = _layer_norm(alpha * x_ref[...] + 0.5 * o_ref[...], g_ref[...], b_ref[...])
        o_ref[...] = y
        if emit_bf16:
            ob_ref[...] = y.astype(BF16)


def _ffn_ln(x, wg, wu, wd, ln_g, ln_b, *, alpha, emit_bf16, name):
    rows, d = x.shape
    f = wg.shape[1]
    tm = _pick_tile(rows, 1024, BF16_SUBLANES)
    tf = min(FFN_HIDDEN_TILE, f)
    n_main = f // tf
    tail = f - n_main * tf
    n_steps = n_main + int(tail > 0)
    row_tile = pl.BlockSpec((tm, d), lambda i, j: (i, 0), pipeline_mode=pl.Buffered(1))
    vec = pl.BlockSpec((1, d), lambda i, j: (0, 0))
    main_j = lambda j: jnp.minimum(j, n_main - 1)
    in_specs = [
        row_tile, vec, vec,
        pl.BlockSpec((d, tf), lambda i, j: (0, main_j(j))),
        pl.BlockSpec((d, tf), lambda i, j: (0, main_j(j))),
        pl.BlockSpec((tf, d), lambda i, j: (main_j(j), 0)),
    ]
    operands = [x, ln_g, ln_b, wg, wu, wd]
    if tail:
        once = dict(pipeline_mode=pl.Buffered(1))
        in_specs += [pl.BlockSpec((d, tail), lambda i, j: (0, 0), **once),
                     pl.BlockSpec((d, tail), lambda i, j: (0, 0), **once),
                     pl.BlockSpec((tail, d), lambda i, j: (0, 0), **once)]
        operands += [wg[:, n_main * tf:], wu[:, n_main * tf:], wd[n_main * tf:, :]]
    out_shape = [jax.ShapeDtypeStruct((rows, d), F32)]
    if emit_bf16:
        out_shape.append(jax.ShapeDtypeStruct((rows, d), BF16))
    res = pl.pallas_call(
        functools.partial(_ffn_ln_kernel, alpha=alpha, n_main=n_main, has_tail=tail > 0, emit_bf16=emit_bf16),
        grid=(rows // tm, n_steps),
        in_specs=in_specs,
        out_specs=[row_tile] * len(out_shape),
        out_shape=out_shape,
        scratch_shapes=[pltpu.VMEM((tm, d), BF16)],
        compiler_params=_params("arbitrary", "arbitrary"),
        name=name,
    )(*operands)
    return res if emit_bf16 else res[0]


def _apply_epilogue(kind, acc):
    if kind == "silu":
        return _silu(acc)
    if kind == "sigmoid":
        return _sigmoid(acc)
    assert kind == "none", kind
    return acc


def _proj_kernel(x_ref, w_ref, *rest, epilogues, tiles_per_group, dk, k_scale):
    o_ref, wb_scr = rest[-2:]

    @pl.when(pl.program_id(1) == 0)
    def _():
        wb_scr[...] = w_ref[...].astype(BF16)

    acc = _dot(x_ref[...], wb_scr[...])
    group = pl.program_id(0) // tiles_per_group
    if epilogues == ("fgate",):
        lb = rest[0][...]
        o_ref[...] = (lb + (1.0 - lb) * _sigmoid(acc)).astype(o_ref.dtype)
    elif epilogues == ("rot_q", "rot_k"):
        cos = rest[0][...]
        sin = rest[1][...]
        scale = jnp.where(group == 0, 1.0, k_scale).astype(F32)
        even = (lax.broadcasted_iota(jnp.int32, cos.shape, 1) & 1) == 0
        for hh in range(acc.shape[1] // dk):
            a = acc[:, hh * dk:(hh + 1) * dk]
            swapped = jnp.where(even, pltpu.roll(a, dk - 1, 1), pltpu.roll(a, 1, 1))
            o_ref[:, hh * dk:(hh + 1) * dk] = ((a * cos + swapped * sin) * scale).astype(o_ref.dtype)
    else:
        kinds = sorted(set(epilogues))
        res = _apply_epilogue(kinds[0], acc)
        for kind in kinds[1:]:
            hit = functools.reduce(jnp.logical_or, [group == g for g, e in enumerate(epilogues) if e == kind])
            res = jnp.where(hit, _apply_epilogue(kind, acc), res)
        o_ref[...] = res.astype(o_ref.dtype)


def _proj(xb, w, col0, gw, *, epilogues, out_dtype, name, extras=(), dk=0, k_scale=1.0):
    rows, d = xb.shape
    ncols = gw * len(epilogues)
    rotary = epilogues[0].startswith("rot")
    tm = _pick_tile(extras[0].shape[0] if rotary else rows, 1024, BF16_SUBLANES)
    tn = _pick_tile(gw, 1024, max(LANES, dk))
    assert col0 % tn == 0 and rows % tm == 0
    in_specs = [
        pl.BlockSpec((tm, d), lambda n, m: (m, 0)),
        pl.BlockSpec((d, tn), lambda n, m: (0, col0 // tn + n)),
    ]
    if rotary:
        tab_blocks = extras[0].shape[0] // tm
        in_specs += [pl.BlockSpec((tm, dk), lambda n, m: (m % tab_blocks, 0))] * 2
    elif epilogues == ("fgate",):
        in_specs += [pl.BlockSpec((1, tn), lambda n, m: (0, n))]
    return pl.pallas_call(
        functools.partial(_proj_kernel, epilogues=epilogues, tiles_per_group=gw // tn, dk=dk, k_scale=k_scale),
        grid=(ncols // tn, rows // tm),
        in_specs=in_specs,
        out_specs=pl.BlockSpec((tm, tn), lambda n, m: (m, n)),
        out_shape=jax.ShapeDtypeStruct((rows, ncols), out_dtype),
        scratch_shapes=[pltpu.VMEM((d, tn), BF16)],
        compiler_params=_params("arbitrary", "arbitrary"),
        name=name,
    )(xb, w, *extras)


def _project_all(xb, w_in, d, cos, sin, lb, dk_ret, out_dtype, tag):
    qk = _proj(xb, w_in, 0, d, epilogues=("rot_q", "rot_k"), out_dtype=out_dtype, extras=(cos, sin),
               dk=dk_ret, k_scale=dk_ret ** -0.5, name=f"proj_qk_{tag}")
    vgq = _proj(xb, w_in, 2 * d, d, epilogues=("none", "silu", "silu"), out_dtype=out_dtype,
                name=f"proj_vgq_{tag}")
    fg = _proj(xb, w_in, 5 * d, d, epilogues=("fgate",), out_dtype=F32, extras=(lb,), name=f"proj_f_{tag}")
    rest = _proj(xb, w_in, 6 * d, d, epilogues=("none", "silu", "sigmoid", "sigmoid"), out_dtype=out_dtype,
                 name=f"proj_rest_{tag}")
    return qk, vgq, fg, rest


def _ret_log2_gamma(h):
    return math.log2(1.0 - 2.0 ** (-5.0 - h))


def _ret_scan_kernel(q_ref, k_ref, v_ref, g_ref, a_ref, km_ref, vm_ref, y_ref, s_ref, s_scr, *,
                     n_heads, dk, dv, chunk, n_meta, n_chunks):
    c = pl.program_id(1)

    @pl.when(c == 0)
    def _():
        mpos = lax.broadcasted_iota(jnp.int32, (n_meta, dk), 0).astype(F32)
        for h in range(n_heads):
            lg = _ret_log2_gamma(h)
            kd = (km_ref[:, h * dk:(h + 1) * dk].astype(F32) * jnp.exp2(lg * (n_meta - 1.0 - mpos))).astype(BF16)
            s_scr[h] = _dot_tn(kd, vm_ref[:, h * dv:(h + 1) * dv].astype(BF16))

    rel = (lax.broadcasted_iota(jnp.int32, (chunk, chunk), 0)
           - lax.broadcasted_iota(jnp.int32, (chunk, chunk), 1)).astype(F32)
    pos_k = lax.broadcasted_iota(jnp.int32, (chunk, dk), 0).astype(F32)
    pos_v = lax.broadcasted_iota(jnp.int32, (chunk, dv), 0).astype(F32)
    for h in range(n_heads):
        lg = _ret_log2_gamma(h)
        ks = slice(h * dk, (h + 1) * dk)
        vs = slice(h * dv, (h + 1) * dv)
        q = q_ref[:, ks]
        k = k_ref[:, ks]
        v = v_ref[:, vs]
        s = s_scr[h]
        decay = jnp.where(rel >= 0, jnp.exp2(lg * jnp.maximum(rel, 0.0)), 0.0)
        inner = (_dot_nt(q, k) * decay).astype(BF16)
        o = _dot(inner, v) + _dot(q, s.astype(BF16)) * jnp.exp2(lg * (pos_v + 1.0))
        kd = (k.astype(F32) * jnp.exp2(lg * (chunk - 1.0 - pos_k))).astype(BF16)
        s_scr[h] = 2.0 ** (lg * chunk) * s + _dot_tn(kd, v)
        mu = jnp.mean(o, axis=-1, keepdims=True)
        dlt = o - mu
        var = jnp.mean(dlt * dlt, axis=-1, keepdims=True)
        gates = g_ref[:, vs].astype(F32) * a_ref[:, vs].astype(F32)
        y_ref[:, vs] = (dlt * lax.rsqrt(var + LN_EPS) * gates).astype(y_ref.dtype)

    @pl.when(c == n_chunks - 1)
    def _():
        s_ref[0, 0] = s_scr[...]


def _ret_scan(qk, vgq, rest, qk_s, vgq_s, *, batch, seq, n_heads, dk, dv, n_meta, meta_row0, depth):
    d = n_heads * dk
    chunk = _pick_tile(seq, RET_CHUNK, BF16_SUBLANES)
    n_chunks = seq // chunk
    assert meta_row0 % n_meta == 0
    mblk = meta_row0 // n_meta
    row = lambda b, c: b * n_chunks + c
    return pl.pallas_call(
        functools.partial(_ret_scan_kernel, n_heads=n_heads, dk=dk, dv=dv, chunk=chunk,
                          n_meta=n_meta, n_chunks=n_chunks),
        grid=(batch, n_chunks),
        in_specs=[
            pl.BlockSpec((chunk, d), lambda b, c: (row(b, c), 0)),
            pl.BlockSpec((chunk, d), lambda b, c: (row(b, c), 1)),
            pl.BlockSpec((chunk, d), lambda b, c: (row(b, c), 0)),
            pl.BlockSpec((chunk, d), lambda b, c: (row(b, c), 1)),
            pl.BlockSpec((chunk, d), lambda b, c: (row(b, c), 2)),
            pl.BlockSpec((n_meta, d), lambda b, c: (mblk, 1)),
            pl.BlockSpec((n_meta, d), lambda b, c: (mblk, 0)),
        ],
        out_specs=[
            pl.BlockSpec((chunk, d), lambda b, c: (row(b, c), 0)),
            pl.BlockSpec((1, 1, n_heads, dk, dv), lambda b, c: (0, b, 0, 0, 0)),
        ],
        out_shape=[
            jax.ShapeDtypeStruct((batch * seq, d), BF16),
            jax.ShapeDtypeStruct((depth, batch, n_heads, dk, dv), F32),
        ],
        scratch_shapes=[pltpu.VMEM((n_heads, dk, dv), F32)],
        compiler_params=_params("arbitrary", "arbitrary"),
        name="ret_scan",
    )(qk, qk, vgq, vgq, rest, qk_s, vgq_s)


def _split_dot(m01, x):
    hi = x.astype(BF16)
    lo = (x - hi.astype(F32)).astype(BF16)
    return _dot(m01, hi) + _dot(m01, lo)


def _level_boundary(b, level, rows):
    n, width = b.shape
    m = 1 << level
    half = m >> 1
    if m >= 8:
        pieces = [jnp.broadcast_to(b[p * m + half - 1:p * m + half, :], (m, width)) for p in range(n // m)]
        return pieces[0] if len(pieces) == 1 else jnp.concatenate(pieces, axis=0)
    if m == 2:
        return jnp.where((rows & 1) == 1, pltpu.roll(b, 1, 0), b)
    r4 = rows & 3
    nxt = pltpu.roll(b, n - 1, 0)
    prv = pltpu.roll(b, 1, 0)
    prv2 = pltpu.roll(b, 2, 0)
    return jnp.where(r4 == 0, nxt, jnp.where(r4 == 1, b, jnp.where(r4 == 2, prv, prv2)))


def _hgrn_scan_kernel(q_ref, f_ref, v_ref, g_ref, a_ref, fm_ref, vm_ref, gn_ref, y_ref, s_ref,
                      st_scr, mask_scr, sign_scr, *, n_heads, dk, dv, chunk, n_meta, n_chunks, n_levels):
    bi = pl.program_id(0)
    c = pl.program_id(1)
    row = lax.broadcasted_iota(jnp.int32, (chunk, chunk), 0)
    col = lax.broadcasted_iota(jnp.int32, (chunk, chunk), 1)
    rows_k = lax.broadcasted_iota(jnp.int32, (chunk, dk), 0)

    @pl.when((bi == 0) & (c == 0))
    def _():
        mask_scr[0] = jnp.where(row == col, 1.0, 0.0)
        for l in range(1, n_levels + 1):
            same_block = (row >> l) == (col >> l)
            pair = (((row >> (l - 1)) & 1) == 1) & (((col >> (l - 1)) & 1) == 0)
            mask_scr[l] = jnp.where(same_block & pair, 1.0, 0.0)
            sign_scr[l - 1] = jnp.where(((rows_k >> (l - 1)) & 1) == 1, 1.0, -1.0)

    @pl.when(c == 0)
    def _():
        mr = lax.broadcasted_iota(jnp.int32, (n_meta, n_meta), 0)
        mc = lax.broadcasted_iota(jnp.int32, (n_meta, n_meta), 1)
        later = jnp.where(mc > mr, 1.0, 0.0).astype(BF16)

        def meta_body(h, carry):
            ks = pl.ds(pl.multiple_of(h * dk, dk), dk)
            vs = pl.ds(pl.multiple_of(h * dv, dv), dv)
            f = fm_ref[:, ks]
            tail = _split_dot(later, jnp.log2(f))
            kd = ((1.0 - f) * jnp.exp2(tail)).astype(BF16)
            st_scr[h] = _dot_tn(vm_ref[:, vs].astype(BF16), kd)
            return carry

        lax.fori_loop(0, n_heads, meta_body, 0)

    ltri = jnp.where(row >= col, 1.0, 0.0).astype(BF16)

    def head_body(h, carry):
        ks = pl.ds(pl.multiple_of(h * dk, dk), dk)
        vs = pl.ds(pl.multiple_of(h * dv, dv), dv)
        f = f_ref[:, ks]
        k = 1.0 - f
        b = _split_dot(ltri, jnp.log2(f))
        qb = q_ref[:, ks]
        q = qb.astype(F32)
        v = v_ref[:, vs]
        att = mask_scr[0] * _dot_nt(qb, k.astype(BF16))
        for l in range(1, n_levels + 1):
            e = jnp.exp2((b - _level_boundary(b, l, rows_k)) * sign_scr[l - 1])
            att = att + mask_scr[l] * _dot_nt((q * e).astype(BF16), (k * e).astype(BF16))
        st = st_scr[h]
        o = _dot(att.astype(BF16), v) + _dot_nt((q * jnp.exp2(b)).astype(BF16), st.astype(BF16))
        btot = b[chunk - 1:chunk, :]
        kd = (k * jnp.exp2(btot - b)).astype(BF16)
        st_scr[h] = jnp.exp2(btot) * st + _dot_tn(v, kd)
        ms = jnp.mean(o * o, axis=-1, keepdims=True)
        gates = g_ref[:, vs].astype(F32) * a_ref[:, vs].astype(F32)
        y_ref[:, vs] = (o * lax.rsqrt(ms + LN_EPS) * gn_ref[:, vs] * gates).astype(y_ref.dtype)
        return carry

    lax.fori_loop(0, n_heads, head_body, 0, unroll=HGRN_HEAD_UNROLL)

    @pl.when(c == n_chunks - 1)
    def _():
        for h in range(n_heads):
            s_ref[0, 0, h] = st_scr[h].T


def _hgrn_scan(vgq, fg, rest, fg_s, rest_s, gn, *, batch, seq, n_heads, dk, dv, n_meta, meta_row0, depth):
    d = n_heads * dk
    chunk = _pick_tile(seq, HGRN_CHUNK, BF16_SUBLANES)
    assert chunk & (chunk - 1) == 0 and chunk >= 8
    n_levels = chunk.bit_length() - 1
    n_chunks = seq // chunk
    mblk = meta_row0 // n_meta
    row = lambda b, c: b * n_chunks + c
    return pl.pallas_call(
        functools.partial(_hgrn_scan_kernel, n_heads=n_heads, dk=dk, dv=dv, chunk=chunk,
                          n_meta=n_meta, n_chunks=n_chunks, n_levels=n_levels),
        grid=(batch, n_chunks),
        in_specs=[
            pl.BlockSpec((chunk, d), lambda b, c: (row(b, c), 2)),
            pl.BlockSpec((chunk, d), lambda b, c: (row(b, c), 0)),
            pl.BlockSpec((chunk, d), lambda b, c: (row(b, c), 0)),
            pl.BlockSpec((chunk, d), lambda b, c: (row(b, c), 1)),
            pl.BlockSpec((chunk, d), lambda b, c: (row(b, c), 3)),
            pl.BlockSpec((n_meta, d), lambda b, c: (mblk, 0)),
            pl.BlockSpec((n_meta, d), lambda b, c: (mblk, 0)),
            pl.BlockSpec((1, d), lambda b, c: (0, 0)),
        ],
        out_specs=[
            pl.BlockSpec((chunk, d), lambda b, c: (row(b, c), 0)),
            pl.BlockSpec((1, 1, n_heads, dk, dv), lambda b, c: (0, b, 0, 0, 0)),
        ],
        out_shape=[
            jax.ShapeDtypeStruct((batch * seq, d), BF16),
            jax.ShapeDtypeStruct((depth, batch, n_heads, dk, dv), F32),
        ],
        scratch_shapes=[
            pltpu.VMEM((n_heads, dv, dk), F32),
            pltpu.VMEM((n_levels + 1, chunk, chunk), F32),
            pltpu.VMEM((n_levels, chunk, dk), F32),
        ],
        compiler_params=_params("arbitrary", "arbitrary"),
        name="hgrn_scan",
    )(vgq, fg, rest, rest, rest, fg_s, rest_s, gn)


def _column_bcast(r, n_rows, n_cols):
    return jnp.broadcast_to(r, (n_cols, n_rows)).T


def _decode_kernel(qk_ref, vgq_ref, f_ref, rest_ref, gn_ref, sr_ref, sh_ref,
                   y_ref, sro_ref, sho_ref, *, d, h_ret, dk_r, dv_r, h_hg, dk_h, dv_h):
    qk = qk_ref[0]
    vgq = vgq_ref[0]
    fg = f_ref[0]
    rest = rest_ref[0]
    gn = gn_ref[...]
    out_r = []
    for h in range(h_ret):
        q = qk[:, h * dk_r:(h + 1) * dk_r]
        k = qk[:, d + h * dk_r:d + (h + 1) * dk_r]
        v = vgq[:, h * dv_r:(h + 1) * dv_r]
        s_new = 2.0 ** _ret_log2_gamma(h) * sr_ref[0, 0, h] + _column_bcast(k, dk_r, dv_r) * v
        sro_ref[0, 0, h] = s_new
        o = _dot(jnp.broadcast_to(q, (8, dk_r)).astype(BF16), s_new.astype(BF16))[0:1, :]
        mu = jnp.mean(o, axis=-1, keepdims=True)
        dlt = o - mu
        var = jnp.mean(dlt * dlt, axis=-1, keepdims=True)
        gates = vgq[:, d + h * dv_r:d + (h + 1) * dv_r] * rest[:, 2 * d + h * dv_r:2 * d + (h + 1) * dv_r]
        out_r.append(dlt * lax.rsqrt(var + LN_EPS) * gates)
    out_h = []
    for h in range(h_hg):
        ks = slice(h * dk_h, (h + 1) * dk_h)
        vs = slice(h * dv_h, (h + 1) * dv_h)
        f = fg[:, ks]
        q = vgq[:, 2 * d + h * dk_h:2 * d + (h + 1) * dk_h]
        v = rest[:, vs]
        s_new = _column_bcast(f, dk_h, dv_h) * sh_ref[0, 0, h] + _column_bcast(1.0 - f, dk_h, dv_h) * v
        sho_ref[0, 0, h] = s_new
        o = _dot(jnp.broadcast_to(q, (8, dk_h)).astype(BF16), s_new.astype(BF16))[0:1, :]
        ms = jnp.mean(o * o, axis=-1, keepdims=True)
        gates = rest[:, d + h * dv_h:d + (h + 1) * dv_h] * rest[:, 3 * d + h * dv_h:3 * d + (h + 1) * dv_h]
        out_h.append(o * lax.rsqrt(ms + LN_EPS) * gn[:, vs] * gates)
    y_ref[0] = jnp.concatenate(out_r, axis=1) + jnp.concatenate(out_h, axis=1)


def _decode(qk_s, vgq_s, fg_s, rest_s, gn, state_ret, state_hgrn, *, n_dec):
    depth, _, h_ret, dk_r, dv_r = state_ret.shape
    _, _, h_hg, dk_h, dv_h = state_hgrn.shape
    d = h_ret * dk_r
    rows = qk_s.shape[0]
    as3d = lambda a: a.reshape(rows, 1, a.shape[1])
    row_spec = lambda w: pl.BlockSpec((1, 1, w), lambda b: (b, 0, 0))
    ret_spec = pl.BlockSpec((1, 1, h_ret, dk_r, dv_r), lambda b: (0, b, 0, 0, 0))
    hg_spec = pl.BlockSpec((1, 1, h_hg, dk_h, dv_h), lambda b: (0, b, 0, 0, 0))
    y, sr, sh = pl.pallas_call(
        functools.partial(_decode_kernel, d=d, h_ret=h_ret, dk_r=dk_r, dv_r=dv_r,
                          h_hg=h_hg, dk_h=dk_h, dv_h=dv_h),
        grid=(n_dec,),
        in_specs=[row_spec(2 * d), row_spec(3 * d), row_spec(d), row_spec(4 * d),
                  pl.BlockSpec((1, d), lambda b: (0, 0)), ret_spec, hg_spec],
        out_specs=[row_spec(d), ret_spec, hg_spec],
        out_shape=[
            jax.ShapeDtypeStruct((n_dec, 1, d), F32),
            jax.ShapeDtypeStruct(state_ret.shape, F32),
            jax.ShapeDtypeStruct(state_hgrn.shape, F32),
        ],
        compiler_params=_params("arbitrary"),
        name="decode",
    )(as3d(qk_s), as3d(vgq_s), as3d(fg_s), as3d(rest_s), gn, state_ret, state_hgrn)
    return y.reshape(n_dec, d), sr, sh


def _wout_ln_kernel(*refs, alpha, n_y):
    y_refs = refs[:n_y]
    x_ref, w_ref, g_ref, b_ref, o_ref = refs[n_y:]
    y = y_refs[0][...].astype(F32)
    for r in y_refs[1:]:
        y = y + r[...].astype(F32)
    m = _dot(y.astype(BF16), w_ref[...])
    o_ref[...] = _layer_norm(alpha * x_ref[...] + m, g_ref[...], b_ref[...])


def _wout_ln(ys, x, w, ln_g, ln_b, *, alpha, name):
    rows, d = x.shape
    tm = _pick_tile(rows, 512, BF16_SUBLANES)
    tile = pl.BlockSpec((tm, d), lambda i: (i, 0))
    vec = pl.BlockSpec((1, d), lambda i: (0, 0))
    return pl.pallas_call(
        functools.partial(_wout_ln_kernel, alpha=alpha, n_y=len(ys)),
        grid=(rows // tm,),
        in_specs=[tile] * len(ys) + [tile, pl.BlockSpec((d, d), lambda i: (0, 0)), vec, vec],
        out_specs=tile,
        out_shape=jax.ShapeDtypeStruct((rows, d), F32),
        compiler_params=_params("arbitrary"),
        name=name,
    )(*ys, x, w, ln_g, ln_b)


def _rotary_tables(pos, dk):
    inv = ROPE_BASE ** (-jnp.arange(0, dk, 2, dtype=F32) / dk)
    ang = pos.astype(F32)[:, None] * inv[None, :]
    cos, sin = jnp.cos(ang), jnp.sin(ang)
    cos_full = jnp.stack([cos, cos], axis=-1).reshape(pos.shape[0], dk)
    sin_signed = jnp.stack([-sin, sin], axis=-1).reshape(pos.shape[0], dk)
    return cos_full, sin_signed


def kernel(x_prompt, x_sample, state_ret, state_hgrn, meta_tokens, ln1_g, ln1_b, ffn1_w_gate, ffn1_w_up, ffn1_w_down, w_in, hgrn_lb_logits, hgrn_norm_g, w_out, ln2_g, ln2_b, ffn2_w_gate, ffn2_w_up, ffn2_w_down, ln3_g, ln3_b):
    batch, seq, d = x_prompt.shape
    n_dec, dec_seq, _ = x_sample.shape
    depth, _, h_ret, dk_r, dv_r = state_ret.shape
    _, _, h_hg, dk_h, dv_h = state_hgrn.shape
    n_meta = meta_tokens.shape[0]
    assert depth == 1 and dec_seq == 1
    alpha = (2.0 * depth) ** 0.25
    layer = 0

    ffn1 = (ffn1_w_gate[layer], ffn1_w_up[layer], ffn1_w_down[layer])
    ffn2 = (ffn2_w_gate[layer], ffn2_w_up[layer], ffn2_w_down[layer])
    w_in_b = w_in[layer]
    w_out_b = w_out[layer].astype(BF16)
    vec = lambda a: a[layer].reshape(1, d)
    lb = jnp.cumsum(jax.nn.softmax(hgrn_lb_logits.astype(F32), axis=0), axis=0)[layer].reshape(1, d)
    gn = vec(hgrn_norm_g)

    xp = x_prompt.reshape(batch * seq, d)
    xs = jnp.concatenate([x_sample.reshape(n_dec, d), meta_tokens.astype(x_prompt.dtype)], axis=0)
    cos_p, sin_p = _rotary_tables(n_meta + jnp.arange(seq, dtype=jnp.int32), dk_r)
    pos_s = jnp.concatenate([jnp.full((n_dec,), PAST_LEN, jnp.int32), jnp.arange(n_meta, dtype=jnp.int32)])
    cos_s, sin_s = _rotary_tables(pos_s, dk_r)

    x1p, x1pb = _ffn_ln(xp, *ffn1, vec(ln1_g), vec(ln1_b), alpha=alpha, emit_bf16=True, name="ffn1_prompt")
    x1s, x1sb = _ffn_ln(xs, *ffn1, vec(ln1_g), vec(ln1_b), alpha=alpha, emit_bf16=True, name="ffn1_small")

    qk_p, vgq_p, fg_p, rest_p = _project_all(x1pb, w_in_b, d, cos_p, sin_p, lb, dk_r, BF16, "prompt")
    qk_s, vgq_s, fg_s, rest_s = _project_all(x1sb, w_in_b, d, cos_s, sin_s, lb, dk_r, F32, "small")

    yr_p, state_ret_prompt = _ret_scan(qk_p, vgq_p, rest_p, qk_s, vgq_s, batch=batch, seq=seq, n_heads=h_ret,
                                       dk=dk_r, dv=dv_r, n_meta=n_meta, meta_row0=n_dec, depth=depth)
    yh_p, state_hgrn_prompt = _hgrn_scan(vgq_p, fg_p, rest_p, fg_s, rest_s, gn, batch=batch, seq=seq,
                                         n_heads=h_hg, dk=dk_h, dv=dv_h, n_meta=n_meta, meta_row0=n_dec,
                                         depth=depth)
    y_s, state_ret_sample, state_hgrn_sample = _decode(qk_s, vgq_s, fg_s, rest_s, gn, state_ret, state_hgrn,
                                                       n_dec=n_dec)

    x2p = _wout_ln([yr_p, yh_p], x1p, w_out_b, vec(ln2_g), vec(ln2_b), alpha=alpha, name="wout_prompt")
    x2s = _wout_ln([y_s], x1s[:n_dec], w_out_b, vec(ln2_g), vec(ln2_b), alpha=alpha, name="wout_small")

    y_prompt = _ffn_ln(x2p, *ffn2, vec(ln3_g), vec(ln3_b), alpha=alpha, emit_bf16=False, name="ffn2_prompt")
    y_sample = _ffn_ln(x2s, *ffn2, vec(ln3_g), vec(ln3_b), alpha=alpha, emit_bf16=False, name="ffn2_small")

    return (y_prompt.reshape(batch, seq, d), y_sample.reshape(n_dec, dec_seq, d), state_ret_prompt,
            state_ret_sample, state_hgrn_prompt, state_hgrn_sample)
```

```python
import functools
import math

import jax
import jax.numpy as jnp
from jax import lax
from jax.experimental import pallas as pl
from jax.experimental.pallas import tpu as pltpu

F32 = jnp.float32
BF16 = jnp.bfloat16

PAST_LEN = 16384
LN_EPS = 1e-5
ROPE_BASE = 10000.0

V7X_VMEM_BYTES = 64 * 1024 * 1024
VMEM_LIMIT_BYTES = V7X_VMEM_BYTES - 8 * 1024 * 1024
LANES = 128
F32_SUBLANES = 8
BF16_SUBLANES = 16

FFN_HIDDEN_TILE = 256
RET_CHUNK = 256
HGRN_CHUNK = 128
DECODE_ROWS_PER_STEP = 2
HGRN_HEAD_UNROLL = 8


def _params(*sem):
    return pltpu.CompilerParams(dimension_semantics=sem, vmem_limit_bytes=VMEM_LIMIT_BYTES)


def _dot(a, b):
    return jnp.dot(a, b, preferred_element_type=F32)


def _dot_nt(a, b):
    return lax.dot_general(a, b, (((1,), (1,)), ((), ())), preferred_element_type=F32)


def _dot_tn(a, b):
    return lax.dot_general(a, b, (((0,), (0,)), ((), ())), preferred_element_type=F32)


def _sigmoid(x):
    return 0.5 * jnp.tanh(0.5 * x) + 0.5


def _silu(x):
    return x * _sigmoid(x)


def _layer_norm(y, g, b):
    mu = jnp.mean(y, axis=-1, keepdims=True)
    d = y - mu
    var = jnp.mean(d * d, axis=-1, keepdims=True)
    return d * lax.rsqrt(var + LN_EPS) * g + b


def _pick_tile(n, target, quantum):
    if n <= target:
        return n
    t = (target // quantum) * quantum
    while t > quantum and n % t:
        t -= quantum
    assert n % t == 0, (n, target, quantum)
    return t


def _ffn_ln_kernel(x_ref, g_ref, b_ref, *rest, alpha, n_main, has_tail, emit_bf16):
    main_w, rest = rest[:3], rest[3:]
    tail_w, rest = (rest[:3], rest[3:]) if has_tail else (None, rest)
    if emit_bf16:
        o_ref, ob_ref, xb_scr = rest
    else:
        (o_ref, xb_scr), ob_ref = rest, None
    j = pl.program_id(1)
    n_steps = n_main + int(has_tail)

    @pl.when(j == 0)
    def _():
        x = x_ref[...]
        xb_scr[...] = x.astype(BF16)
        o_ref[...] = (2.0 * alpha) * x

    def accumulate(wg_ref, wu_ref, wd_ref):
        xb = xb_scr[...]
        gate = _dot(xb, wg_ref[...].astype(BF16))
        up = _dot(xb, wu_ref[...].astype(BF16))
        h = (_silu(gate) * up).astype(BF16)
        o_ref[...] += _dot(h, wd_ref[...].astype(BF16))

    if has_tail:
        pl.when(j < n_main)(functools.partial(accumulate, *main_w))
        pl.when(j == n_main)(functools.partial(accumulate, *tail_w))
    else:
        accumulate(*main_w)

    @pl.when(j == n_steps - 1)
    def _():
        y = _layer_norm(0.5 * o_ref[...], g_ref[...], b_ref[...])
        o_ref[...] = y
        if emit_bf16:
            ob_ref[...] = y.astype(BF16)


def _ffn_ln(x, wg, wu, wd, ln_g, ln_b, *, alpha, emit_bf16, name):
    rows, d = x.shape
    f = wg.shape[1]
    tm = _pick_tile(rows, 1024, BF16_SUBLANES)
    tf = min(FFN_HIDDEN_TILE, f)
    n_main = f // tf
    tail = f - n_main * tf
    n_steps = n_main + int(tail > 0)
    row_tile = pl.BlockSpec((tm, d), lambda i, j: (i, 0), pipeline_mode=pl.Buffered(1))
    vec = pl.BlockSpec((1, d), lambda i, j: (0, 0))
    main_j = lambda j: jnp.minimum(j, n_main - 1)
    in_specs = [
        pl.BlockSpec((tm, d), lambda i, j: (i, 0)), vec, vec,
        pl.BlockSpec((d, tf), lambda i, j: (0, main_j(j))),
        pl.BlockSpec((d, tf), lambda i, j: (0, main_j(j))),
        pl.BlockSpec((tf, d), lambda i, j: (main_j(j), 0)),
    ]
    operands = [x, ln_g, ln_b, wg, wu, wd]
    if tail:
        once = dict(pipeline_mode=pl.Buffered(1))
        in_specs += [pl.BlockSpec((d, tail), lambda i, j: (0, 0), **once),
                     pl.BlockSpec((d, tail), lambda i, j: (0, 0), **once),
                     pl.BlockSpec((tail, d), lambda i, j: (0, 0), **once)]
        operands += [wg[:, n_main * tf:], wu[:, n_main * tf:], wd[n_main * tf:, :]]
    out_shape = [jax.ShapeDtypeStruct((rows, d), F32)]
    if emit_bf16:
        out_shape.append(jax.ShapeDtypeStruct((rows, d), BF16))
    res = pl.pallas_call(
        functools.partial(_ffn_ln_kernel, alpha=alpha, n_main=n_main, has_tail=tail > 0, emit_bf16=emit_bf16),
        grid=(rows // tm, n_steps),
        in_specs=in_specs,
        out_specs=[row_tile] * len(out_shape),
        out_shape=out_shape,
        scratch_shapes=[pltpu.VMEM((tm, d), BF16)],
        compiler_params=_params("arbitrary", "arbitrary"),
        name=name,
    )(*operands)
    return res if emit_bf16 else res[0]


def _apply_epilogue(kind, acc):
    if kind == "silu":
        return _silu(acc)
    if kind == "sigmoid":
        return _sigmoid(acc)
    assert kind == "none", kind
    return acc


def _proj_kernel(x_ref, w_ref, *rest, epilogues, tiles_per_group, dk, k_scale):
    o_ref, wb_scr = rest[-2:]

    @pl.when(pl.program_id(1) == 0)
    def _():
        wb_scr[...] = w_ref[...].astype(BF16)

    acc = _dot(x_ref[...], wb_scr[...])
    group = pl.program_id(0) // tiles_per_group
    if epilogues == ("fgate",):
        lb = rest[0][...]
        o_ref[...] = (lb + (1.0 - lb) * _sigmoid(acc)).astype(o_ref.dtype)
    elif epilogues == ("rot_q", "rot_k"):
        cos = rest[0][...]
        sin = rest[1][...]
        scale = jnp.where(group == 0, 1.0, k_scale).astype(F32)
        even = (lax.broadcasted_iota(jnp.int32, cos.shape, 1) & 1) == 0
        for hh in range(acc.shape[1] // dk):
            a = acc[:, hh * dk:(hh + 1) * dk]
            swapped = jnp.where(even, pltpu.roll(a, dk - 1, 1), pltpu.roll(a, 1, 1))
            o_ref[:, hh * dk:(hh + 1) * dk] = ((a * cos + swapped * sin) * scale).astype(o_ref.dtype)
    else:
        kinds = sorted(set(epilogues))
        res = _apply_epilogue(kinds[0], acc)
        for kind in kinds[1:]:
            hit = functools.reduce(jnp.logical_or, [group == g for g, e in enumerate(epilogues) if e == kind])
            res = jnp.where(hit, _apply_epilogue(kind, acc), res)
        o_ref[...] = res.astype(o_ref.dtype)


def _proj(xb, w, col0, gw, *, epilogues, out_dtype, name, extras=(), dk=0, k_scale=1.0):
    rows, d = xb.shape
    ncols = gw * len(epilogues)
    rotary = epilogues[0].startswith("rot")
    tm = _pick_tile(extras[0].shape[0] if rotary else rows, 1024, BF16_SUBLANES)
    tn = _pick_tile(gw, 1024, max(LANES, dk))
    assert col0 % tn == 0 and rows % tm == 0
    in_specs = [
        pl.BlockSpec((tm, d), lambda n, m: (m, 0)),
        pl.BlockSpec((d, tn), lambda n, m: (0, col0 // tn + n)),
    ]
    if rotary:
        tab_blocks = extras[0].shape[0] // tm
        in_specs += [pl.BlockSpec((tm, dk), lambda n, m: (m % tab_blocks, 0))] * 2
    elif epilogues == ("fgate",):
        in_specs += [pl.BlockSpec((1, tn), lambda n, m: (0, n))]
    return pl.pallas_call(
        functools.partial(_proj_kernel, epilogues=epilogues, tiles_per_group=gw // tn, dk=dk, k_scale=k_scale),
        grid=(ncols // tn, rows // tm),
        in_specs=in_specs,
        out_specs=pl.BlockSpec((tm, tn), lambda n, m: (m, n)),
        out_shape=jax.ShapeDtypeStruct((rows, ncols), out_dtype),
        scratch_shapes=[pltpu.VMEM((d, tn), BF16)],
        compiler_params=_params("arbitrary", "arbitrary"),
        name=name,
    )(xb, w, *extras)


def _project_all(xb, w_in, d, cos, sin, lb, dk_ret, out_dtype, tag):
    qk = _proj(xb, w_in, 0, d, epilogues=("rot_q", "rot_k"), out_dtype=out_dtype, extras=(cos, sin),
               dk=dk_ret, k_scale=dk_ret ** -0.5, name=f"proj_qk_{tag}")
    vgq = _proj(xb, w_in, 2 * d, d, epilogues=("none", "silu", "silu"), out_dtype=out_dtype,
                name=f"proj_vgq_{tag}")
    fg = _proj(xb, w_in, 5 * d, d, epilogues=("fgate",), out_dtype=F32, extras=(lb,), name=f"proj_f_{tag}")
    rest = _proj(xb, w_in, 6 * d, d, epilogues=("none", "silu", "sigmoid", "sigmoid"), out_dtype=out_dtype,
                 name=f"proj_rest_{tag}")
    return qk, vgq, fg, rest


def _ret_log_gamma(h):
    return math.log(1.0 - 2.0 ** (-5.0 - h))


def _ret_scan_kernel(q_ref, k_ref, v_ref, g_ref, a_ref, km_ref, vm_ref, y_ref, s_ref, s_scr, *,
                     n_heads, dk, dv, chunk, n_meta, n_chunks):
    c = pl.program_id(1)

    @pl.when(c == 0)
    def _():
        mpos = lax.broadcasted_iota(jnp.int32, (n_meta, dk), 0).astype(F32)
        for h in range(n_heads):
            lg = _ret_log_gamma(h)
            kd = (km_ref[:, h * dk:(h + 1) * dk].astype(F32) * jnp.exp(lg * (n_meta - 1.0 - mpos))).astype(BF16)
            s_scr[h] = _dot_tn(kd, vm_ref[:, h * dv:(h + 1) * dv].astype(BF16))

    rel = (lax.broadcasted_iota(jnp.int32, (chunk, chunk), 0)
           - lax.broadcasted_iota(jnp.int32, (chunk, chunk), 1)).astype(F32)
    pos_k = lax.broadcasted_iota(jnp.int32, (chunk, dk), 0).astype(F32)
    pos_v = lax.broadcasted_iota(jnp.int32, (chunk, dv), 0).astype(F32)
    for h in range(n_heads):
        lg = _ret_log_gamma(h)
        ks = slice(h * dk, (h + 1) * dk)
        vs = slice(h * dv, (h + 1) * dv)
        q = q_ref[:, ks]
        k = k_ref[:, ks]
        v = v_ref[:, vs]
        s = s_scr[h]
        decay = jnp.where(rel >= 0, jnp.exp(lg * jnp.maximum(rel, 0.0)), 0.0)
        inner = (_dot_nt(q, k) * decay).astype(BF16)
        o = _dot(inner, v) + _dot(q, s.astype(BF16)) * jnp.exp(lg * (pos_v + 1.0))
        kd = (k.astype(F32) * jnp.exp(lg * (chunk - 1.0 - pos_k))).astype(BF16)
        s_scr[h] = math.exp(lg * chunk) * s + _dot_tn(kd, v)
        mu = jnp.mean(o, axis=-1, keepdims=True)
        dlt = o - mu
        var = jnp.mean(dlt * dlt, axis=-1, keepdims=True)
        gates = g_ref[:, vs].astype(F32) * a_ref[:, vs].astype(F32)
        y_ref[:, vs] = (dlt * lax.rsqrt(var + LN_EPS) * gates).astype(y_ref.dtype)

    @pl.when(c == n_chunks - 1)
    def _():
        s_ref[0, 0] = s_scr[...]


def _ret_scan(qk, vgq, rest, qk_s, vgq_s, *, batch, seq, n_heads, dk, dv, n_meta, meta_row0, depth):
    d = n_heads * dk
    chunk = _pick_tile(seq, RET_CHUNK, BF16_SUBLANES)
    n_chunks = seq // chunk
    assert meta_row0 % n_meta == 0
    mblk = meta_row0 // n_meta
    row = lambda b, c: b * n_chunks + c
    return pl.pallas_call(
        functools.partial(_ret_scan_kernel, n_heads=n_heads, dk=dk, dv=dv, chunk=chunk,
                          n_meta=n_meta, n_chunks=n_chunks),
        grid=(batch, n_chunks),
        in_specs=[
            pl.BlockSpec((chunk, d), lambda b, c: (row(b, c), 0)),
            pl.BlockSpec((chunk, d), lambda b, c: (row(b, c), 1)),
            pl.BlockSpec((chunk, d), lambda b, c: (row(b, c), 0)),
            pl.BlockSpec((chunk, d), lambda b, c: (row(b, c), 1)),
            pl.BlockSpec((chunk, d), lambda b, c: (row(b, c), 2)),
            pl.BlockSpec((n_meta, d), lambda b, c: (mblk, 1)),
            pl.BlockSpec((n_meta, d), lambda b, c: (mblk, 0)),
        ],
        out_specs=[
            pl.BlockSpec((chunk, d), lambda b, c: (row(b, c), 0)),
            pl.BlockSpec((1, 1, n_heads, dk, dv), lambda b, c: (0, b, 0, 0, 0)),
        ],
        out_shape=[
            jax.ShapeDtypeStruct((batch * seq, d), BF16),
            jax.ShapeDtypeStruct((depth, batch, n_heads, dk, dv), F32),
        ],
        scratch_shapes=[pltpu.VMEM((n_heads, dk, dv), F32)],
        compiler_params=_params("arbitrary", "arbitrary"),
        name="ret_scan",
    )(qk, qk, vgq, vgq, rest, qk_s, vgq_s)


def _split_dot(m01, x):
    hi = x.astype(BF16)
    lo = (x - hi.astype(F32)).astype(BF16)
    return _dot(m01, hi) + _dot(m01, lo)


def _hgrn_scan_kernel(q_ref, f_ref, v_ref, g_ref, a_ref, fm_ref, vm_ref, gn_ref, y_ref, s_ref,
                      st_scr, mask_scr, sums_scr, logsum_scr, *, n_heads, dk, dv, chunk, n_meta, n_chunks,
                      n_levels, n_small):
    bi = pl.program_id(0)
    c = pl.program_id(1)
    row = lax.broadcasted_iota(jnp.int32, (chunk, chunk), 0)
    col = lax.broadcasted_iota(jnp.int32, (chunk, chunk), 1)

    @pl.when((bi == 0) & (c == 0))
    def _():
        mask_scr[0] = jnp.where(row == col, 1.0, 0.0).astype(BF16)
        sums_scr[0:chunk] = jnp.where(row >= col, 1.0, 0.0).astype(BF16)
        for l in range(1, n_levels + 1):
            same_block = (row >> l) == (col >> l)
            pair = (((row >> (l - 1)) & 1) == 1) & (((col >> (l - 1)) & 1) == 0)
            mask_scr[l] = jnp.where(same_block & pair, 1.0, 0.0).astype(BF16)
            if l <= n_small:
                bnd = ((row >> l) << l) + ((1 << (l - 1)) - 1)
                second = ((row >> (l - 1)) & 1) == 1
                between = (second & (col > bnd) & (col <= row)) | (~second & (col > row) & (col <= bnd))
                sums_scr[l * chunk:(l + 1) * chunk] = jnp.where(between, 1.0, 0.0).astype(BF16)

    @pl.when(c == 0)
    def _():
        mr = lax.broadcasted_iota(jnp.int32, (n_meta, n_meta), 0)
        mc = lax.broadcasted_iota(jnp.int32, (n_meta, n_meta), 1)
        later = jnp.where(mc > mr, 1.0, 0.0).astype(BF16)

        def meta_body(h, carry):
            ks = pl.ds(pl.multiple_of(h * dk, dk), dk)
            vs = pl.ds(pl.multiple_of(h * dv, dv), dv)
            f = fm_ref[:, ks]
            tail = _split_dot(later, jnp.log(f))
            kd = ((1.0 - f) * jnp.exp(tail)).astype(BF16)
            st_scr[h] = _dot_tn(vm_ref[:, vs].astype(BF16), kd)
            return carry

        lax.fori_loop(0, n_heads, meta_body, 0)

    logf = jnp.log(f_ref[...])
    logf_hi = logf.astype(BF16)
    logf_lo = (logf - logf_hi.astype(F32)).astype(BF16)
    logsum_scr[0:chunk] = _dot(sums_scr[0:chunk], logf_hi) + _dot(sums_scr[0:chunk], logf_lo)
    if n_small:
        logsum_scr[chunk:] = _dot(sums_scr[chunk:], logf_hi)

    def head_body(h, carry):
        ks = pl.ds(pl.multiple_of(h * dk, dk), dk)
        vs = pl.ds(pl.multiple_of(h * dv, dv), dv)
        k = 1.0 - f_ref[:, ks]
        kb = k.astype(BF16)
        b = logsum_scr[0:chunk, ks]
        qb = q_ref[:, ks]
        q = qb.astype(F32)
        v = v_ref[:, vs]
        att = mask_scr[0] * _dot_nt(qb, kb).astype(BF16)
        for l in range(1, n_levels + 1):
            m = 1 << l
            half = m >> 1
            if l <= n_small:
                e = jnp.exp(logsum_scr[l * chunk:(l + 1) * chunk, ks]).astype(BF16)
                prod = _dot_nt(qb * e, kb * e)
            else:
                q_rows, k_rows = [], []
                zero = jnp.zeros((half, dk), F32)
                for p in range(chunk // m):
                    r = b[p * m + half - 1:p * m + half, :]
                    lo = slice(p * m, p * m + half)
                    hi = slice(p * m + half, (p + 1) * m)
                    k_rows += [k[lo] * jnp.exp(r - b[lo]), zero]
                    q_rows += [zero, q[hi] * jnp.exp(b[hi] - r)]
                prod = _dot_nt(jnp.concatenate(q_rows, axis=0).astype(BF16),
                               jnp.concatenate(k_rows, axis=0).astype(BF16))
            prod = prod.astype(BF16)
            att = att + (prod if m == chunk else mask_scr[l] * prod)
        st = st_scr[h]
        o = _dot(att, v) + _dot_nt((q * jnp.exp(b)).astype(BF16), st.astype(BF16))
        btot = b[chunk - 1:chunk, :]
        kd = (k * jnp.exp(btot - b)).astype(BF16)
        st_scr[h] = jnp.exp(btot) * st + _dot_tn(v, kd)
        ms = jnp.mean(o * o, axis=-1, keepdims=True)
        gates = g_ref[:, vs].astype(F32) * a_ref[:, vs].astype(F32)
        y_ref[:, vs] = (o * lax.rsqrt(ms + LN_EPS) * gn_ref[:, vs] * gates).astype(y_ref.dtype)
        return carry

    lax.fori_loop(0, n_heads, head_body, 0, unroll=HGRN_HEAD_UNROLL)

    @pl.when(c == n_chunks - 1)
    def _():
        for h in range(n_heads):
            s_ref[0, 0, h] = st_scr[h].T


def _hgrn_scan(vgq, fg, rest, fg_s, rest_s, gn, *, batch, seq, n_heads, dk, dv, n_meta, meta_row0, depth):
    d = n_heads * dk
    chunk = _pick_tile(seq, HGRN_CHUNK, BF16_SUBLANES)
    assert chunk & (chunk - 1) == 0 and chunk >= 8
    n_levels = chunk.bit_length() - 1
    n_small = min(n_levels, F32_SUBLANES.bit_length() - 1)
    n_chunks = seq // chunk
    mblk = meta_row0 // n_meta
    row = lambda b, c: b * n_chunks + c
    return pl.pallas_call(
        functools.partial(_hgrn_scan_kernel, n_heads=n_heads, dk=dk, dv=dv, chunk=chunk,
                          n_meta=n_meta, n_chunks=n_chunks, n_levels=n_levels, n_small=n_small),
        grid=(batch, n_chunks),
        in_specs=[
            pl.BlockSpec((chunk, d), lambda b, c: (row(b, c), 2)),
            pl.BlockSpec((chunk, d), lambda b, c: (row(b, c), 0)),
            pl.BlockSpec((chunk, d), lambda b, c: (row(b, c), 0)),
            pl.BlockSpec((chunk, d), lambda b, c: (row(b, c), 1)),
            pl.BlockSpec((chunk, d), lambda b, c: (row(b, c), 3)),
            pl.BlockSpec((n_meta, d), lambda b, c: (mblk, 0)),
            pl.BlockSpec((n_meta, d), lambda b, c: (mblk, 0)),
            pl.BlockSpec((1, d), lambda b, c: (0, 0)),
        ],
        out_specs=[
            pl.BlockSpec((chunk, d), lambda b, c: (row(b, c), 0)),
            pl.BlockSpec((1, 1, n_heads, dk, dv), lambda b, c: (0, b, 0, 0, 0)),
        ],
        out_shape=[
            jax.ShapeDtypeStruct((batch * seq, d), BF16),
            jax.ShapeDtypeStruct((depth, batch, n_heads, dk, dv), F32),
        ],
        scratch_shapes=[
            pltpu.VMEM((n_heads, dv, dk), F32),
            pltpu.VMEM((n_levels + 1, chunk, chunk), BF16),
            pltpu.VMEM(((n_small + 1) * chunk, chunk), BF16),
            pltpu.VMEM(((n_small + 1) * chunk, d), F32),
        ],
        compiler_params=_params("arbitrary", "arbitrary"),
        name="hgrn_scan",
    )(vgq, fg, rest, rest, rest, fg_s, rest_s, gn)


def _column_bcast(r, n_rows, n_cols):
    return jnp.broadcast_to(r, (n_cols, n_rows)).T


def _decode_kernel(qk_ref, vgq_ref, f_ref, rest_ref, gn_ref, sr_ref, sh_ref,
                   y_ref, sro_ref, sho_ref, *, per_step, **dims):
    for e in range(per_step):
        _decode_one(e, qk_ref, vgq_ref, f_ref, rest_ref, gn_ref, sr_ref, sh_ref, y_ref, sro_ref, sho_ref, **dims)


def _decode_one(e, qk_ref, vgq_ref, f_ref, rest_ref, gn_ref, sr_ref, sh_ref,
                y_ref, sro_ref, sho_ref, *, d, h_ret, dk_r, dv_r, h_hg, dk_h, dv_h):
    qk = qk_ref[e]
    vgq = vgq_ref[e]
    fg = f_ref[e]
    rest = rest_ref[e]
    gn = gn_ref[...]
    out_r = []
    for h in range(h_ret):
        q = qk[:, h * dk_r:(h + 1) * dk_r]
        k = qk[:, d + h * dk_r:d + (h + 1) * dk_r]
        v = vgq[:, h * dv_r:(h + 1) * dv_r]
        s_new = math.exp(_ret_log_gamma(h)) * sr_ref[0, e, h] + _column_bcast(k, dk_r, dv_r) * v
        sro_ref[0, e, h] = s_new
        o = _dot(jnp.broadcast_to(q, (8, dk_r)).astype(BF16), s_new.astype(BF16))[0:1, :]
        mu = jnp.mean(o, axis=-1, keepdims=True)
        dlt = o - mu
        var = jnp.mean(dlt * dlt, axis=-1, keepdims=True)
        gates = vgq[:, d + h * dv_r:d + (h + 1) * dv_r] * rest[:, 2 * d + h * dv_r:2 * d + (h + 1) * dv_r]
        out_r.append(dlt * lax.rsqrt(var + LN_EPS) * gates)
    out_h = []
    for h in range(h_hg):
        ks = slice(h * dk_h, (h + 1) * dk_h)
        vs = slice(h * dv_h, (h + 1) * dv_h)
        f = fg[:, ks]
        q = vgq[:, 2 * d + h * dk_h:2 * d + (h + 1) * dk_h]
        v = rest[:, vs]
        s_new = _column_bcast(f, dk_h, dv_h) * sh_ref[0, e, h] + _column_bcast(1.0 - f, dk_h, dv_h) * v
        sho_ref[0, e, h] = s_new
        o = _dot(jnp.broadcast_to(q, (8, dk_h)).astype(BF16), s_new.astype(BF16))[0:1, :]
        ms = jnp.mean(o * o, axis=-1, keepdims=True)
        gates = rest[:, d + h * dv_h:d + (h + 1) * dv_h] * rest[:, 3 * d + h * dv_h:3 * d + (h + 1) * dv_h]
        out_h.append(o * lax.rsqrt(ms + LN_EPS) * gn[:, vs] * gates)
    y_ref[e] = jnp.concatenate(out_r, axis=1) + jnp.concatenate(out_h, axis=1)


def _decode(qk_s, vgq_s, fg_s, rest_s, gn, state_ret, state_hgrn, *, n_dec):
    depth, _, h_ret, dk_r, dv_r = state_ret.shape
    _, _, h_hg, dk_h, dv_h = state_hgrn.shape
    d = h_ret * dk_r
    rows = qk_s.shape[0]
    per = DECODE_ROWS_PER_STEP if n_dec % DECODE_ROWS_PER_STEP == 0 else 1
    as3d = lambda a: a.reshape(rows, 1, a.shape[1])
    row_spec = lambda w: pl.BlockSpec((per, 1, w), lambda b: (b, 0, 0))
    ret_spec = pl.BlockSpec((1, per, h_ret, dk_r, dv_r), lambda b: (0, b, 0, 0, 0))
    hg_spec = pl.BlockSpec((1, per, h_hg, dk_h, dv_h), lambda b: (0, b, 0, 0, 0))
    y, sr, sh = pl.pallas_call(
        functools.partial(_decode_kernel, per_step=per, d=d, h_ret=h_ret, dk_r=dk_r, dv_r=dv_r,
                          h_hg=h_hg, dk_h=dk_h, dv_h=dv_h),
        grid=(n_dec // per,),
        in_specs=[row_spec(2 * d), row_spec(3 * d), row_spec(d), row_spec(4 * d),
                  pl.BlockSpec((1, d), lambda b: (0, 0)), ret_spec, hg_spec],
        out_specs=[row_spec(d), ret_spec, hg_spec],
        out_shape=[
            jax.ShapeDtypeStruct((n_dec, 1, d), F32),
            jax.ShapeDtypeStruct(state_ret.shape, F32),
            jax.ShapeDtypeStruct(state_hgrn.shape, F32),
        ],
        compiler_params=_params("arbitrary"),
        name="decode",
    )(as3d(qk_s), as3d(vgq_s), as3d(fg_s), as3d(rest_s), gn, state_ret, state_hgrn)
    return y.reshape(n_dec, d), sr, sh


def _wout_ln_kernel(*refs, alpha, n_y):
    y_refs = refs[:n_y]
    x_ref, w_ref, g_ref, b_ref, o_ref = refs[n_y:]
    y = y_refs[0][...].astype(F32)
    for r in y_refs[1:]:
        y = y + r[...].astype(F32)
    m = _dot(y.astype(BF16), w_ref[...])
    o_ref[...] = _layer_norm(alpha * x_ref[...] + m, g_ref[...], b_ref[...])


def _wout_ln(ys, x, w, ln_g, ln_b, *, alpha, name):
    rows, d = x.shape
    tm = _pick_tile(rows, 512, BF16_SUBLANES)
    tile = pl.BlockSpec((tm, d), lambda i: (i, 0))
    vec = pl.BlockSpec((1, d), lambda i: (0, 0))
    return pl.pallas_call(
        functools.partial(_wout_ln_kernel, alpha=alpha, n_y=len(ys)),
        grid=(rows // tm,),
        in_specs=[tile] * len(ys) + [tile, pl.BlockSpec((d, d), lambda i: (0, 0)), vec, vec],
        out_specs=tile,
        out_shape=jax.ShapeDtypeStruct((rows, d), F32),
        compiler_params=_params("arbitrary"),
        name=name,
    )(*ys, x, w, ln_g, ln_b)


def _rotary_tables(pos, dk):
    inv = ROPE_BASE ** (-jnp.arange(0, dk, 2, dtype=F32) / dk)
    ang = pos.astype(F32)[:, None] * inv[None, :]
    cos, sin = jnp.cos(ang), jnp.sin(ang)
    cos_full = jnp.stack([cos, cos], axis=-1).reshape(pos.shape[0], dk)
    sin_signed = jnp.stack([-sin, sin], axis=-1).reshape(pos.shape[0], dk)
    return cos_full, sin_signed


def kernel(x_prompt, x_sample, state_ret, state_hgrn, meta_tokens, ln1_g, ln1_b, ffn1_w_gate, ffn1_w_up, ffn1_w_down, w_in, hgrn_lb_logits, hgrn_norm_g, w_out, ln2_g, ln2_b, ffn2_w_gate, ffn2_w_up, ffn2_w_down, ln3_g, ln3_b):
    batch, seq, d = x_prompt.shape
    n_dec, dec_seq, _ = x_sample.shape
    depth, _, h_ret, dk_r, dv_r = state_ret.shape
    _, _, h_hg, dk_h, dv_h = state_hgrn.shape
    n_meta = meta_tokens.shape[0]
    assert depth == 1 and dec_seq == 1
    alpha = (2.0 * depth) ** 0.25
    layer = 0

    ffn1 = (ffn1_w_gate[layer], ffn1_w_up[layer], ffn1_w_down[layer])
    ffn2 = (ffn2_w_gate[layer], ffn2_w_up[layer], ffn2_w_down[layer])
    w_in_b = w_in[layer]
    w_out_b = w_out[layer].astype(BF16)
    vec = lambda a: a[layer].reshape(1, d)
    lb = jnp.cumsum(jax.nn.softmax(hgrn_lb_logits.astype(F32), axis=0), axis=0)[layer].reshape(1, d)
    gn = vec(hgrn_norm_g)

    xp = x_prompt.reshape(batch * seq, d)
    xs = jnp.concatenate([x_sample.reshape(n_dec, d), meta_tokens.astype(x_prompt.dtype)], axis=0)
    cos_p, sin_p = _rotary_tables(n_meta + jnp.arange(seq, dtype=jnp.int32), dk_r)
    pos_s = jnp.concatenate([jnp.full((n_dec,), PAST_LEN, jnp.int32), jnp.arange(n_meta, dtype=jnp.int32)])
    cos_s, sin_s = _rotary_tables(pos_s, dk_r)

    x1p, x1pb = _ffn_ln(xp, *ffn1, vec(ln1_g), vec(ln1_b), alpha=alpha, emit_bf16=True, name="ffn1_prompt")
    x1s, x1sb = _ffn_ln(xs, *ffn1, vec(ln1_g), vec(ln1_b), alpha=alpha, emit_bf16=True, name="ffn1_small")

    qk_p, vgq_p, fg_p, rest_p = _project_all(x1pb, w_in_b, d, cos_p, sin_p, lb, dk_r, BF16, "prompt")
    qk_s, vgq_s, fg_s, rest_s = _project_all(x1sb, w_in_b, d, cos_s, sin_s, lb, dk_r, F32, "small")

    yr_p, state_ret_prompt = _ret_scan(qk_p, vgq_p, rest_p, qk_s, vgq_s, batch=batch, seq=seq, n_heads=h_ret,
                                       dk=dk_r, dv=dv_r, n_meta=n_meta, meta_row0=n_dec, depth=depth)
    yh_p, state_hgrn_prompt = _hgrn_scan(vgq_p, fg_p, rest_p, fg_s, rest_s, gn, batch=batch, seq=seq,
                                         n_heads=h_hg, dk=dk_h, dv=dv_h, n_meta=n_meta, meta_row0=n_dec,
                                         depth=depth)
    y_s, state_ret_sample, state_hgrn_sample = _decode(qk_s, vgq_s, fg_s, rest_s, gn, state_ret, state_hgrn,
                                                       n_dec=n_dec)

    x2p = _wout_ln([yr_p, yh_p], x1p, w_out_b, vec(ln2_g), vec(ln2_b), alpha=alpha, name="wout_prompt")
    x2s = _wout_ln([y_s], x1s[:n_dec], w_out_b, vec(ln2_g), vec(ln2_b), alpha=alpha, name="wout_small")

    y_prompt = _ffn_ln(x2p, *ffn2, vec(ln3_g), vec(ln3_b), alpha=alpha, emit_bf16=False, name="ffn2_prompt")
    y_sample = _ffn_ln(x2s, *ffn2, vec(ln3_g), vec(ln3_b), alpha=alpha, emit_bf16=False, name="ffn2_small")

    return (y_prompt.reshape(batch, seq, d), y_sample.reshape(n_dec, dec_seq, d), state_ret_prompt,
            state_ret_sample, state_hgrn_prompt, state_hgrn_sample)
```

```python
import functools
import math

import jax
import jax.numpy as jnp
from jax import lax
from jax.experimental import pallas as pl
from jax.experimental.pallas import tpu as pltpu

F32 = jnp.float32
BF16 = jnp.bfloat16

PAST_LEN = 16384
LN_EPS = 1e-5
ROPE_BASE = 10000.0

V7X_VMEM_BYTES = 64 * 1024 * 1024
VMEM_LIMIT_BYTES = V7X_VMEM_BYTES - 8 * 1024 * 1024
LANES = 128
F32_SUBLANES = 8
BF16_SUBLANES = 16

FFN_HIDDEN_TILE = 256
PROJ_ROW_TILE = 2048
PROJ_VMEM_BUDGET_BYTES = 46 * 1024 * 1024
RET_CHUNK = 256
HGRN_CHUNK = 128
DECODE_ROWS_PER_STEP = 2
HGRN_HEAD_UNROLL = 8


def _params(*sem):
    return pltpu.CompilerParams(dimension_semantics=sem, vmem_limit_bytes=VMEM_LIMIT_BYTES)


def _dot(a, b):
    return jnp.dot(a, b, preferred_element_type=F32)


def _dot_nt(a, b):
    return lax.dot_general(a, b, (((1,), (1,)), ((), ())), preferred_element_type=F32)


def _dot_tn(a, b):
    return lax.dot_general(a, b, (((0,), (0,)), ((), ())), preferred_element_type=F32)


def _sigmoid(x):
    return 0.5 * jnp.tanh(0.5 * x) + 0.5


def _silu(x):
    return x * _sigmoid(x)


def _layer_norm(y, g, b):
    mu = jnp.mean(y, axis=-1, keepdims=True)
    d = y - mu
    var = jnp.mean(d * d, axis=-1, keepdims=True)
    return d * lax.rsqrt(var + LN_EPS) * g + b


def _pick_tile(n, target, quantum):
    if n <= target:
        return n
    t = (target // quantum) * quantum
    while t > quantum and n % t:
        t -= quantum
    assert n % t == 0, (n, target, quantum)
    return t


def _ffn_ln_kernel(x_ref, g_ref, b_ref, *rest, alpha, n_main, has_tail, emit_bf16):
    main_w, rest = rest[:3], rest[3:]
    tail_w, rest = (rest[:3], rest[3:]) if has_tail else (None, rest)
    if emit_bf16:
        o_ref, ob_ref, xb_scr = rest
    else:
        (o_ref, xb_scr), ob_ref = rest, None
    j = pl.program_id(1)
    n_steps = n_main + int(has_tail)

    @pl.when(j == 0)
    def _():
        x = x_ref[...]
        xb_scr[...] = x.astype(BF16)
        o_ref[...] = (2.0 * alpha) * x

    def accumulate(wg_ref, wu_ref, wd_ref):
        xb = xb_scr[...]
        gate = _dot(xb, wg_ref[...].astype(BF16))
        up = _dot(xb, wu_ref[...].astype(BF16))
        h = (_silu(gate) * up).astype(BF16)
        o_ref[...] += _dot(h, wd_ref[...].astype(BF16))

    if has_tail:
        pl.when(j < n_main)(functools.partial(accumulate, *main_w))
        pl.when(j == n_main)(functools.partial(accumulate, *tail_w))
    else:
        accumulate(*main_w)

    @pl.when(j == n_steps - 1)
    def _():
        y = _layer_norm(0.5 * o_ref[...], g_ref[...], b_ref[...])
        o_ref[...] = y
        if emit_bf16:
            ob_ref[...] = y.astype(BF16)


def _ffn_ln(x, wg, wu, wd, ln_g, ln_b, *, alpha, emit_bf16, name):
    rows, d = x.shape
    f = wg.shape[1]
    tm = _pick_tile(rows, 1024, BF16_SUBLANES)
    tf = min(FFN_HIDDEN_TILE, f)
    n_main = f // tf
    tail = f - n_main * tf
    n_steps = n_main + int(tail > 0)
    row_tile = pl.BlockSpec((tm, d), lambda i, j: (i, 0), pipeline_mode=pl.Buffered(1))
    vec = pl.BlockSpec((1, d), lambda i, j: (0, 0))
    main_j = lambda j: jnp.minimum(j, n_main - 1)
    in_specs = [
        pl.BlockSpec((tm, d), lambda i, j: (i, 0)), vec, vec,
        pl.BlockSpec((d, tf), lambda i, j: (0, main_j(j))),
        pl.BlockSpec((d, tf), lambda i, j: (0, main_j(j))),
        pl.BlockSpec((tf, d), lambda i, j: (main_j(j), 0)),
    ]
    operands = [x, ln_g, ln_b, wg, wu, wd]
    if tail:
        once = dict(pipeline_mode=pl.Buffered(1))
        in_specs += [pl.BlockSpec((d, tail), lambda i, j: (0, 0), **once),
                     pl.BlockSpec((d, tail), lambda i, j: (0, 0), **once),
                     pl.BlockSpec((tail, d), lambda i, j: (0, 0), **once)]
        operands += [wg[:, n_main * tf:], wu[:, n_main * tf:], wd[n_main * tf:, :]]
    out_shape = [jax.ShapeDtypeStruct((rows, d), F32)]
    if emit_bf16:
        out_shape.append(jax.ShapeDtypeStruct((rows, d), BF16))
    res = pl.pallas_call(
        functools.partial(_ffn_ln_kernel, alpha=alpha, n_main=n_main, has_tail=tail > 0, emit_bf16=emit_bf16),
        grid=(rows // tm, n_steps),
        in_specs=in_specs,
        out_specs=[row_tile] * len(out_shape),
        out_shape=out_shape,
        scratch_shapes=[pltpu.VMEM((tm, d), BF16)],
        compiler_params=_params("arbitrary", "arbitrary"),
        name=name,
    )(*operands)
    return res if emit_bf16 else res[0]


def _apply_epilogue(kind, acc):
    if kind == "silu":
        return _silu(acc)
    if kind == "sigmoid":
        return _sigmoid(acc)
    assert kind == "none", kind
    return acc


def _proj_kernel(x_ref, w_ref, *rest, epilogues, tiles_per_group, dk, k_scale):
    o_ref, wb_scr = rest[-2:]

    @pl.when(pl.program_id(1) == 0)
    def _():
        wb_scr[...] = w_ref[...].astype(BF16)

    acc = _dot(x_ref[...], wb_scr[...])
    group = pl.program_id(0) // tiles_per_group
    if epilogues == ("fgate",):
        lb = rest[0][...]
        o_ref[...] = (lb + (1.0 - lb) * _sigmoid(acc)).astype(o_ref.dtype)
    elif epilogues == ("rot_q", "rot_k"):
        cos = rest[0][...]
        sin = rest[1][...]
        scale = jnp.where(group == 0, 1.0, k_scale).astype(F32)
        even = (lax.broadcasted_iota(jnp.int32, cos.shape, 1) & 1) == 0
        for hh in range(acc.shape[1] // dk):
            a = acc[:, hh * dk:(hh + 1) * dk]
            swapped = jnp.where(even, pltpu.roll(a, dk - 1, 1), pltpu.roll(a, 1, 1))
            o_ref[:, hh * dk:(hh + 1) * dk] = ((a * cos + swapped * sin) * scale).astype(o_ref.dtype)
    else:
        kinds = sorted(set(epilogues))
        res = _apply_epilogue(kinds[0], acc)
        for kind in kinds[1:]:
            hit = functools.reduce(jnp.logical_or, [group == g for g, e in enumerate(epilogues) if e == kind])
            res = jnp.where(hit, _apply_epilogue(kind, acc), res)
        o_ref[...] = res.astype(o_ref.dtype)


def _proj(xb, w, col0, gw, *, epilogues, out_dtype, name, extras=(), dk=0, k_scale=1.0):
    rows, d = xb.shape
    ncols = gw * len(epilogues)
    rotary = epilogues[0].startswith("rot")
    tn = _pick_tile(gw, 1024, max(LANES, dk))
    out_bytes = jnp.dtype(out_dtype).itemsize

    def vmem_bytes(tm):
        tables = 2 * 2 * tm * dk * 4 if rotary else 0
        return 2 * tm * d * 2 + 2 * tm * tn * out_bytes + 2 * d * tn * 4 + d * tn * 2 + tables

    tm_target = PROJ_ROW_TILE
    while tm_target > 256 and vmem_bytes(tm_target) > PROJ_VMEM_BUDGET_BYTES:
        tm_target //= 2
    tm = _pick_tile(extras[0].shape[0] if rotary else rows, tm_target, BF16_SUBLANES)
    assert col0 % tn == 0 and rows % tm == 0
    in_specs = [
        pl.BlockSpec((tm, d), lambda n, m: (m, 0)),
        pl.BlockSpec((d, tn), lambda n, m: (0, col0 // tn + n)),
    ]
    if rotary:
        tab_blocks = extras[0].shape[0] // tm
        in_specs += [pl.BlockSpec((tm, dk), lambda n, m: (m % tab_blocks, 0))] * 2
    elif epilogues == ("fgate",):
        in_specs += [pl.BlockSpec((1, tn), lambda n, m: (0, n))]
    return pl.pallas_call(
        functools.partial(_proj_kernel, epilogues=epilogues, tiles_per_group=gw // tn, dk=dk, k_scale=k_scale),
        grid=(ncols // tn, rows // tm),
        in_specs=in_specs,
        out_specs=pl.BlockSpec((tm, tn), lambda n, m: (m, n)),
        out_shape=jax.ShapeDtypeStruct((rows, ncols), out_dtype),
        scratch_shapes=[pltpu.VMEM((d, tn), BF16)],
        compiler_params=_params("arbitrary", "arbitrary"),
        name=name,
    )(xb, w, *extras)


def _project_all(xb, w_in, d, cos, sin, lb, dk_ret, out_dtype, tag):
    qk = _proj(xb, w_in, 0, d, epilogues=("rot_q", "rot_k"), out_dtype=out_dtype, extras=(cos, sin),
               dk=dk_ret, k_scale=dk_ret ** -0.5, name=f"proj_qk_{tag}")
    vgq = _proj(xb, w_in, 2 * d, d, epilogues=("none", "silu", "silu"), out_dtype=out_dtype,
                name=f"proj_vgq_{tag}")
    fg = _proj(xb, w_in, 5 * d, d, epilogues=("fgate",), out_dtype=F32, extras=(lb,), name=f"proj_f_{tag}")
    rest = _proj(xb, w_in, 6 * d, d, epilogues=("none", "silu", "sigmoid", "sigmoid"), out_dtype=out_dtype,
                 name=f"proj_rest_{tag}")
    return qk, vgq, fg, rest


def _ret_log_gamma(h):
    return math.log(1.0 - 2.0 ** (-5.0 - h))


def _ret_scan_kernel(q_ref, k_ref, v_ref, g_ref, a_ref, km_ref, vm_ref, y_ref, s_ref, s_scr, *,
                     n_heads, dk, dv, chunk, n_meta, n_chunks):
    c = pl.program_id(1)

    @pl.when(c == 0)
    def _():
        mpos = lax.broadcasted_iota(jnp.int32, (n_meta, dk), 0).astype(F32)
        for h in range(n_heads):
            lg = _ret_log_gamma(h)
            kd = (km_ref[:, h * dk:(h + 1) * dk].astype(F32) * jnp.exp(lg * (n_meta - 1.0 - mpos))).astype(BF16)
            s_scr[h] = _dot_tn(kd, vm_ref[:, h * dv:(h + 1) * dv].astype(BF16))

    rel = (lax.broadcasted_iota(jnp.int32, (chunk, chunk), 0)
           - lax.broadcasted_iota(jnp.int32, (chunk, chunk), 1)).astype(F32)
    pos_k = lax.broadcasted_iota(jnp.int32, (chunk, dk), 0).astype(F32)
    pos_v = lax.broadcasted_iota(jnp.int32, (chunk, dv), 0).astype(F32)
    for h in range(n_heads):
        lg = _ret_log_gamma(h)
        ks = slice(h * dk, (h + 1) * dk)
        vs = slice(h * dv, (h + 1) * dv)
        q = q_ref[:, ks]
        k = k_ref[:, ks]
        v = v_ref[:, vs]
        s = s_scr[h]
        decay = jnp.where(rel >= 0, jnp.exp(lg * jnp.maximum(rel, 0.0)), 0.0)
        inner = (_dot_nt(q, k) * decay).astype(BF16)
        o = _dot(inner, v) + _dot(q, s.astype(BF16)) * jnp.exp(lg * (pos_v + 1.0))
        kd = (k.astype(F32) * jnp.exp(lg * (chunk - 1.0 - pos_k))).astype(BF16)
        s_scr[h] = math.exp(lg * chunk) * s + _dot_tn(kd, v)
        mu = jnp.mean(o, axis=-1, keepdims=True)
        dlt = o - mu
        var = jnp.mean(dlt * dlt, axis=-1, keepdims=True)
        gates = g_ref[:, vs].astype(F32) * a_ref[:, vs].astype(F32)
        y_ref[:, vs] = (dlt * lax.rsqrt(var + LN_EPS) * gates).astype(y_ref.dtype)

    @pl.when(c == n_chunks - 1)
    def _():
        s_ref[0, 0] = s_scr[...]


def _ret_scan(qk, vgq, rest, qk_s, vgq_s, *, batch, seq, n_heads, dk, dv, n_meta, meta_row0, depth):
    d = n_heads * dk
    chunk = _pick_tile(seq, RET_CHUNK, BF16_SUBLANES)
    n_chunks = seq // chunk
    assert meta_row0 % n_meta == 0
    mblk = meta_row0 // n_meta
    row = lambda b, c: b * n_chunks + c
    return pl.pallas_call(
        functools.partial(_ret_scan_kernel, n_heads=n_heads, dk=dk, dv=dv, chunk=chunk,
                          n_meta=n_meta, n_chunks=n_chunks),
        grid=(batch, n_chunks),
        in_specs=[
            pl.BlockSpec((chunk, d), lambda b, c: (row(b, c), 0)),
            pl.BlockSpec((chunk, d), lambda b, c: (row(b, c), 1)),
            pl.BlockSpec((chunk, d), lambda b, c: (row(b, c), 0)),
            pl.BlockSpec((chunk, d), lambda b, c: (row(b, c), 1)),
            pl.BlockSpec((chunk, d), lambda b, c: (row(b, c), 2)),
            pl.BlockSpec((n_meta, d), lambda b, c: (mblk, 1)),
            pl.BlockSpec((n_meta, d), lambda b, c: (mblk, 0)),
        ],
        out_specs=[
            pl.BlockSpec((chunk, d), lambda b, c: (row(b, c), 0)),
            pl.BlockSpec((1, 1, n_heads, dk, dv), lambda b, c: (0, b, 0, 0, 0)),
        ],
        out_shape=[
            jax.ShapeDtypeStruct((batch * seq, d), BF16),
            jax.ShapeDtypeStruct((depth, batch, n_heads, dk, dv), F32),
        ],
        scratch_shapes=[pltpu.VMEM((n_heads, dk, dv), F32)],
        compiler_params=_params("arbitrary", "arbitrary"),
        name="ret_scan",
    )(qk, qk, vgq, vgq, rest, qk_s, vgq_s)


def _split_dot(m01, x):
    hi = x.astype(BF16)
    lo = (x - hi.astype(F32)).astype(BF16)
    return _dot(m01, hi) + _dot(m01, lo)


def _hgrn_scan_kernel(q_ref, f_ref, v_ref, g_ref, a_ref, fm_ref, vm_ref, gn_ref, *rest,
                      n_heads, dk, dv, chunk, n_meta, n_chunks, n_levels, n_small, decode_per_step, decode_dims):
    if decode_per_step:
        dec_in, rest = rest[:6], rest[6:]
        (y_ref, s_ref), dec_out, rest = rest[:2], rest[2:5], rest[5:]
    else:
        (y_ref, s_ref), rest = rest[:2], rest[2:]
    st_scr, mask_scr, sums_scr, logsum_scr = rest
    bi = pl.program_id(0)
    c = pl.program_id(1)
    row = lax.broadcasted_iota(jnp.int32, (chunk, chunk), 0)
    col = lax.broadcasted_iota(jnp.int32, (chunk, chunk), 1)

    @pl.when((bi == 0) & (c == 0))
    def _():
        mask_scr[0] = jnp.where(row == col, 1.0, 0.0).astype(BF16)
        sums_scr[0:chunk] = jnp.where(row >= col, 1.0, 0.0).astype(BF16)
        for l in range(1, n_levels + 1):
            same_block = (row >> l) == (col >> l)
            pair = (((row >> (l - 1)) & 1) == 1) & (((col >> (l - 1)) & 1) == 0)
            mask_scr[l] = jnp.where(same_block & pair, 1.0, 0.0).astype(BF16)
            if l <= n_small:
                bnd = ((row >> l) << l) + ((1 << (l - 1)) - 1)
                second = ((row >> (l - 1)) & 1) == 1
                between = (second & (col > bnd) & (col <= row)) | (~second & (col > row) & (col <= bnd))
                sums_scr[l * chunk:(l + 1) * chunk] = jnp.where(between, 1.0, 0.0).astype(BF16)

    @pl.when(c == 0)
    def _():
        mr = lax.broadcasted_iota(jnp.int32, (n_meta, n_meta), 0)
        mc = lax.broadcasted_iota(jnp.int32, (n_meta, n_meta), 1)
        later = jnp.where(mc > mr, 1.0, 0.0).astype(BF16)

        def meta_body(h, carry):
            ks = pl.ds(pl.multiple_of(h * dk, dk), dk)
            vs = pl.ds(pl.multiple_of(h * dv, dv), dv)
            f = fm_ref[:, ks]
            tail = _split_dot(later, jnp.log(f))
            kd = ((1.0 - f) * jnp.exp(tail)).astype(BF16)
            st_scr[h] = _dot_tn(vm_ref[:, vs].astype(BF16), kd)
            return carry

        lax.fori_loop(0, n_heads, meta_body, 0)

    logf = jnp.log(f_ref[...])
    logf_hi = logf.astype(BF16)
    logf_lo = (logf - logf_hi.astype(F32)).astype(BF16)
    logsum_scr[0:chunk] = _dot(sums_scr[0:chunk], logf_hi) + _dot(sums_scr[0:chunk], logf_lo)
    if n_small:
        logsum_scr[chunk:] = _dot(sums_scr[chunk:], logf_hi)

    def head_body(h):
        ks = pl.ds(pl.multiple_of(h * dk, dk), dk)
        vs = pl.ds(pl.multiple_of(h * dv, dv), dv)
        k = 1.0 - f_ref[:, ks]
        kb = k.astype(BF16)
        b = logsum_scr[0:chunk, ks]
        qb = q_ref[:, ks]
        q = qb.astype(F32)
        v = v_ref[:, vs]
        att = mask_scr[0] * _dot_nt(qb, kb).astype(BF16)
        for l in range(1, n_levels + 1):
            m = 1 << l
            half = m >> 1
            if l <= n_small:
                e = jnp.exp(logsum_scr[l * chunk:(l + 1) * chunk, ks]).astype(BF16)
                prod = _dot_nt(qb * e, kb * e)
            else:
                q_rows, k_rows = [], []
                zero = jnp.zeros((half, dk), F32)
                for p in range(chunk // m):
                    r = b[p * m + half - 1:p * m + half, :]
                    lo = slice(p * m, p * m + half)
                    hi = slice(p * m + half, (p + 1) * m)
                    k_rows += [k[lo] * jnp.exp(r - b[lo]), zero]
                    q_rows += [zero, q[hi] * jnp.exp(b[hi] - r)]
                prod = _dot_nt(jnp.concatenate(q_rows, axis=0).astype(BF16),
                               jnp.concatenate(k_rows, axis=0).astype(BF16))
            prod = prod.astype(BF16)
            att = att + (prod if m == chunk else mask_scr[l] * prod)
        st = st_scr[h]
        o = _dot(att, v) + _dot_nt((q * jnp.exp(b)).astype(BF16), st.astype(BF16))
        btot = b[chunk - 1:chunk, :]
        kd = (k * jnp.exp(btot - b)).astype(BF16)
        st_scr[h] = jnp.exp(btot) * st + _dot_tn(v, kd)
        ms = jnp.mean(o * o, axis=-1, keepdims=True)
        gates = g_ref[:, vs].astype(F32) * a_ref[:, vs].astype(F32)
        y_ref[:, vs] = (o * lax.rsqrt(ms + LN_EPS) * gn_ref[:, vs] * gates).astype(y_ref.dtype)

    n_trips = decode_per_step or n_heads // HGRN_HEAD_UNROLL
    heads_per_trip = n_heads // n_trips

    def trip(t, carry):
        for hh in range(heads_per_trip):
            head_body(t * heads_per_trip + hh)
        if decode_per_step:
            _decode_one(t, *dec_in[:4], gn_ref, *dec_in[4:], *dec_out, **decode_dims)
        return carry

    lax.fori_loop(0, n_trips, trip, 0)

    @pl.when(c == n_chunks - 1)
    def _():
        for h in range(n_heads):
            s_ref[0, 0, h] = st_scr[h].T


def _decode_fits_scan(n_dec, batch, seq, n_heads):
    steps = batch * (seq // _pick_tile(seq, HGRN_CHUNK, BF16_SUBLANES))
    per = n_dec // steps
    return per if per and per * steps == n_dec and n_heads % per == 0 else 0


def _hgrn_scan(vgq, fg, rest, fg_s, rest_s, gn, *, batch, seq, n_heads, dk, dv, n_meta, meta_row0, depth,
               decode=None):
    d = n_heads * dk
    chunk = _pick_tile(seq, HGRN_CHUNK, BF16_SUBLANES)
    assert chunk & (chunk - 1) == 0 and chunk >= 8
    n_levels = chunk.bit_length() - 1
    n_small = min(n_levels, F32_SUBLANES.bit_length() - 1)
    n_chunks = seq // chunk
    mblk = meta_row0 // n_meta
    row = lambda b, c: b * n_chunks + c
    in_specs = [
        pl.BlockSpec((chunk, d), lambda b, c: (row(b, c), 2)),
        pl.BlockSpec((chunk, d), lambda b, c: (row(b, c), 0)),
        pl.BlockSpec((chunk, d), lambda b, c: (row(b, c), 0)),
        pl.BlockSpec((chunk, d), lambda b, c: (row(b, c), 1)),
        pl.BlockSpec((chunk, d), lambda b, c: (row(b, c), 3)),
        pl.BlockSpec((n_meta, d), lambda b, c: (mblk, 0)),
        pl.BlockSpec((n_meta, d), lambda b, c: (mblk, 0)),
        pl.BlockSpec((1, d), lambda b, c: (0, 0)),
    ]
    operands = [vgq, fg, rest, rest, rest, fg_s, rest_s, gn]
    out_specs = [
        pl.BlockSpec((chunk, d), lambda b, c: (row(b, c), 0)),
        pl.BlockSpec((1, 1, n_heads, dk, dv), lambda b, c: (0, b, 0, 0, 0)),
    ]
    out_shape = [
        jax.ShapeDtypeStruct((batch * seq, d), BF16),
        jax.ShapeDtypeStruct((depth, batch, n_heads, dk, dv), F32),
    ]
    per, decode_dims = 0, None
    if decode is not None:
        qk_s, vgq_s, state_ret, state_hgrn, per = decode
        _, n_dec, h_ret, dk_r, dv_r = state_ret.shape
        decode_dims = dict(d=d, h_ret=h_ret, dk_r=dk_r, dv_r=dv_r, h_hg=n_heads, dk_h=dk, dv_h=dv)
        as3d = lambda a: a.reshape(a.shape[0], 1, a.shape[1])
        row_spec = lambda w: pl.BlockSpec((per, 1, w), lambda b, c: (row(b, c), 0, 0))
        ret_spec = pl.BlockSpec((1, per, h_ret, dk_r, dv_r), lambda b, c: (0, row(b, c), 0, 0, 0))
        hg_spec = pl.BlockSpec((1, per, n_heads, dk, dv), lambda b, c: (0, row(b, c), 0, 0, 0))
        in_specs += [row_spec(2 * d), row_spec(3 * d), row_spec(d), row_spec(4 * d), ret_spec, hg_spec]
        operands += [as3d(qk_s), as3d(vgq_s), as3d(fg_s), as3d(rest_s), state_ret, state_hgrn]
        out_specs += [row_spec(d), ret_spec, hg_spec]
        out_shape += [jax.ShapeDtypeStruct((n_dec, 1, d), F32), jax.ShapeDtypeStruct(state_ret.shape, F32),
                      jax.ShapeDtypeStruct(state_hgrn.shape, F32)]
    return pl.pallas_call(
        functools.partial(_hgrn_scan_kernel, n_heads=n_heads, dk=dk, dv=dv, chunk=chunk,
                          n_meta=n_meta, n_chunks=n_chunks, n_levels=n_levels, n_small=n_small,
                          decode_per_step=per, decode_dims=decode_dims),
        grid=(batch, n_chunks),
        in_specs=in_specs,
        out_specs=out_specs,
        out_shape=out_shape,
        scratch_shapes=[
            pltpu.VMEM((n_heads, dv, dk), F32),
            pltpu.VMEM((n_levels + 1, chunk, chunk), BF16),
            pltpu.VMEM(((n_small + 1) * chunk, chunk), BF16),
            pltpu.VMEM(((n_small + 1) * chunk, d), F32),
        ],
        compiler_params=_params("arbitrary", "arbitrary"),
        name="hgrn_scan",
    )(*operands)


def _column_bcast(r, n_rows, n_cols):
    return jnp.broadcast_to(r, (n_cols, n_rows)).T


def _decode_kernel(qk_ref, vgq_ref, f_ref, rest_ref, gn_ref, sr_ref, sh_ref,
                   y_ref, sro_ref, sho_ref, *, per_step, **dims):
    for e in range(per_step):
        _decode_one(e, qk_ref, vgq_ref, f_ref, rest_ref, gn_ref, sr_ref, sh_ref, y_ref, sro_ref, sho_ref, **dims)


def _decode_one(e, qk_ref, vgq_ref, f_ref, rest_ref, gn_ref, sr_ref, sh_ref,
                y_ref, sro_ref, sho_ref, *, d, h_ret, dk_r, dv_r, h_hg, dk_h, dv_h):
    qk = qk_ref[e]
    vgq = vgq_ref[e]
    fg = f_ref[e]
    rest = rest_ref[e]
    gn = gn_ref[...]
    out_r = []
    for h in range(h_ret):
        q = qk[:, h * dk_r:(h + 1) * dk_r]
        k = qk[:, d + h * dk_r:d + (h + 1) * dk_r]
        v = vgq[:, h * dv_r:(h + 1) * dv_r]
        s_new = math.exp(_ret_log_gamma(h)) * sr_ref[0, e, h] + _column_bcast(k, dk_r, dv_r) * v
        sro_ref[0, e, h] = s_new
        o = _dot(jnp.broadcast_to(q, (8, dk_r)).astype(BF16), s_new.astype(BF16))[0:1, :]
        mu = jnp.mean(o, axis=-1, keepdims=True)
        dlt = o - mu
        var = jnp.mean(dlt * dlt, axis=-1, keepdims=True)
        gates = vgq[:, d + h * dv_r:d + (h + 1) * dv_r] * rest[:, 2 * d + h * dv_r:2 * d + (h + 1) * dv_r]
        out_r.append(dlt * lax.rsqrt(var + LN_EPS) * gates)
    out_h = []
    for h in range(h_hg):
        ks = slice(h * dk_h, (h + 1) * dk_h)
        vs = slice(h * dv_h, (h + 1) * dv_h)
        f = fg[:, ks]
        q = vgq[:, 2 * d + h * dk_h:2 * d + (h + 1) * dk_h]
        v = rest[:, vs]
        s_new = _column_bcast(f, dk_h, dv_h) * sh_ref[0, e, h] + _column_bcast(1.0 - f, dk_h, dv_h) * v
        sho_ref[0, e, h] = s_new
        o = _dot(jnp.broadcast_to(q, (8, dk_h)).astype(BF16), s_new.astype(BF16))[0:1, :]
        ms = jnp.mean(o * o, axis=-1, keepdims=True)
        gates = rest[:, d + h * dv_h:d + (h + 1) * dv_h] * rest[:, 3 * d + h * dv_h:3 * d + (h + 1) * dv_h]
        out_h.append(o * lax.rsqrt(ms + LN_EPS) * gn[:, vs] * gates)
    y_ref[e] = jnp.concatenate(out_r, axis=1) + jnp.concatenate(out_h, axis=1)


def _decode(qk_s, vgq_s, fg_s, rest_s, gn, state_ret, state_hgrn, *, n_dec):
    depth, _, h_ret, dk_r, dv_r = state_ret.shape
    _, _, h_hg, dk_h, dv_h = state_hgrn.shape
    d = h_ret * dk_r
    rows = qk_s.shape[0]
    per = DECODE_ROWS_PER_STEP if n_dec % DECODE_ROWS_PER_STEP == 0 else 1
    as3d = lambda a: a.reshape(rows, 1, a.shape[1])
    row_spec = lambda w: pl.BlockSpec((per, 1, w), lambda b: (b, 0, 0))
    ret_spec = pl.BlockSpec((1, per, h_ret, dk_r, dv_r), lambda b: (0, b, 0, 0, 0))
    hg_spec = pl.BlockSpec((1, per, h_hg, dk_h, dv_h), lambda b: (0, b, 0, 0, 0))
    y, sr, sh = pl.pallas_call(
        functools.partial(_decode_kernel, per_step=per, d=d, h_ret=h_ret, dk_r=dk_r, dv_r=dv_r,
                          h_hg=h_hg, dk_h=dk_h, dv_h=dv_h),
        grid=(n_dec // per,),
        in_specs=[row_spec(2 * d), row_spec(3 * d), row_spec(d), row_spec(4 * d),
                  pl.BlockSpec((1, d), lambda b: (0, 0)), ret_spec, hg_spec],
        out_specs=[row_spec(d), ret_spec, hg_spec],
        out_shape=[
            jax.ShapeDtypeStruct((n_dec, 1, d), F32),
            jax.ShapeDtypeStruct(state_ret.shape, F32),
            jax.ShapeDtypeStruct(state_hgrn.shape, F32),
        ],
        compiler_params=_params("arbitrary"),
        name="decode",
    )(as3d(qk_s), as3d(vgq_s), as3d(fg_s), as3d(rest_s), gn, state_ret, state_hgrn)
    return y.reshape(n_dec, d), sr, sh


def _wout_ln_kernel(*refs, alpha, n_y):
    y_refs = refs[:n_y]
    x_ref, w_ref, g_ref, b_ref, o_ref = refs[n_y:]
    y = y_refs[0][...].astype(F32)
    for r in y_refs[1:]:
        y = y + r[...].astype(F32)
    m = _dot(y.astype(BF16), w_ref[...])
    o_ref[...] = _layer_norm(alpha * x_ref[...] + m, g_ref[...], b_ref[...])


def _wout_ln(ys, x, w, ln_g, ln_b, *, alpha, name):
    rows, d = x.shape
    tm = _pick_tile(rows, 512, BF16_SUBLANES)
    tile = pl.BlockSpec((tm, d), lambda i: (i, 0))
    vec = pl.BlockSpec((1, d), lambda i: (0, 0))
    return pl.pallas_call(
        functools.partial(_wout_ln_kernel, alpha=alpha, n_y=len(ys)),
        grid=(rows // tm,),
        in_specs=[tile] * len(ys) + [tile, pl.BlockSpec((d, d), lambda i: (0, 0)), vec, vec],
        out_specs=tile,
        out_shape=jax.ShapeDtypeStruct((rows, d), F32),
        compiler_params=_params("arbitrary"),
        name=name,
    )(*ys, x, w, ln_g, ln_b)


def _rotary_tables(pos, dk):
    inv = ROPE_BASE ** (-jnp.arange(0, dk, 2, dtype=F32) / dk)
    ang = pos.astype(F32)[:, None] * inv[None, :]
    cos, sin = jnp.cos(ang), jnp.sin(ang)
    cos_full = jnp.stack([cos, cos], axis=-1).reshape(pos.shape[0], dk)
    sin_signed = jnp.stack([-sin, sin], axis=-1).reshape(pos.shape[0], dk)
    return cos_full, sin_signed


def kernel(x_prompt, x_sample, state_ret, state_hgrn, meta_tokens, ln1_g, ln1_b, ffn1_w_gate, ffn1_w_up, ffn1_w_down, w_in, hgrn_lb_logits, hgrn_norm_g, w_out, ln2_g, ln2_b, ffn2_w_gate, ffn2_w_up, ffn2_w_down, ln3_g, ln3_b):
    batch, seq, d = x_prompt.shape
    n_dec, dec_seq, _ = x_sample.shape
    depth, _, h_ret, dk_r, dv_r = state_ret.shape
    _, _, h_hg, dk_h, dv_h = state_hgrn.shape
    n_meta = meta_tokens.shape[0]
    assert depth == 1 and dec_seq == 1
    alpha = (2.0 * depth) ** 0.25
    layer = 0

    ffn1 = (ffn1_w_gate[layer], ffn1_w_up[layer], ffn1_w_down[layer])
    ffn2 = (ffn2_w_gate[layer], ffn2_w_up[layer], ffn2_w_down[layer])
    w_in_b = w_in[layer]
    w_out_b = w_out[layer].astype(BF16)
    vec = lambda a: a[layer].reshape(1, d)
    lb = jnp.cumsum(jax.nn.softmax(hgrn_lb_logits.astype(F32), axis=0), axis=0)[layer].reshape(1, d)
    gn = vec(hgrn_norm_g)

    xp = x_prompt.reshape(batch * seq, d)
    xs = jnp.concatenate([x_sample.reshape(n_dec, d), meta_tokens.astype(x_prompt.dtype)], axis=0)
    cos_p, sin_p = _rotary_tables(n_meta + jnp.arange(seq, dtype=jnp.int32), dk_r)
    pos_s = jnp.concatenate([jnp.full((n_dec,), PAST_LEN, jnp.int32), jnp.arange(n_meta, dtype=jnp.int32)])
    cos_s, sin_s = _rotary_tables(pos_s, dk_r)

    x1p, x1pb = _ffn_ln(xp, *ffn1, vec(ln1_g), vec(ln1_b), alpha=alpha, emit_bf16=True, name="ffn1_prompt")
    x1s, x1sb = _ffn_ln(xs, *ffn1, vec(ln1_g), vec(ln1_b), alpha=alpha, emit_bf16=True, name="ffn1_small")

    qk_p, vgq_p, fg_p, rest_p = _project_all(x1pb, w_in_b, d, cos_p, sin_p, lb, dk_r, BF16, "prompt")
    qk_s, vgq_s, fg_s, rest_s = _project_all(x1sb, w_in_b, d, cos_s, sin_s, lb, dk_r, F32, "small")

    yr_p, state_ret_prompt = _ret_scan(qk_p, vgq_p, rest_p, qk_s, vgq_s, batch=batch, seq=seq, n_heads=h_ret,
                                       dk=dk_r, dv=dv_r, n_meta=n_meta, meta_row0=n_dec, depth=depth)
    hgrn_args = dict(batch=batch, seq=seq, n_heads=h_hg, dk=dk_h, dv=dv_h, n_meta=n_meta, meta_row0=n_dec,
                     depth=depth)
    per = _decode_fits_scan(n_dec, batch, seq, h_hg)
    if per:
        yh_p, state_hgrn_prompt, y_s, state_ret_sample, state_hgrn_sample = _hgrn_scan(
            vgq_p, fg_p, rest_p, fg_s, rest_s, gn, decode=(qk_s, vgq_s, state_ret, state_hgrn, per), **hgrn_args)
        y_s = y_s.reshape(n_dec, d)
    else:
        yh_p, state_hgrn_prompt = _hgrn_scan(vgq_p, fg_p, rest_p, fg_s, rest_s, gn, **hgrn_args)
        y_s, state_ret_sample, state_hgrn_sample = _decode(qk_s, vgq_s, fg_s, rest_s, gn, state_ret,
                                                           state_hgrn, n_dec=n_dec)

    x2p = _wout_ln([yr_p, yh_p], x1p, w_out_b, vec(ln2_g), vec(ln2_b), alpha=alpha, name="wout_prompt")
    x2s = _wout_ln([y_s], x1s[:n_dec], w_out_b, vec(ln2_g), vec(ln2_b), alpha=alpha, name="wout_small")

    y_prompt = _ffn_ln(x2p, *ffn2, vec(ln3_g), vec(ln3_b), alpha=alpha, emit_bf16=False, name="ffn2_prompt")
    y_sample = _ffn_ln(x2s, *ffn2, vec(ln3_g), vec(ln3_b), alpha=alpha, emit_bf16=False, name="ffn2_small")

    return (y_prompt.reshape(batch, seq, d), y_sample.reshape(n_dec, dec_seq, d), state_ret_prompt,
            state_ret_sample, state_hgrn_prompt, state_hgrn_sample)
```

```python
import functools
import math

import jax
import jax.numpy as jnp
from jax import lax
from jax.experimental import pallas as pl
from jax.experimental.pallas import tpu as pltpu

F32 = jnp.float32
BF16 = jnp.bfloat16

PAST_LEN = 16384
LN_EPS = 1e-5
ROPE_BASE = 10000.0

V7X_VMEM_BYTES = 64 * 1024 * 1024
VMEM_LIMIT_BYTES = V7X_VMEM_BYTES - 8 * 1024 * 1024
LANES = 128
F32_SUBLANES = 8
BF16_SUBLANES = 16

ROW_TILE = 1024
FFN_HIDDEN_TILE = 256
RET_CHUNK = 256
HGRN_CHUNK = 128
DECODE_ROWS_PER_STEP = 2
HGRN_HEAD_UNROLL = 8


def _params(*sem):
    return pltpu.CompilerParams(dimension_semantics=sem, vmem_limit_bytes=VMEM_LIMIT_BYTES)


def _dot(a, b):
    return jnp.dot(a, b, preferred_element_type=F32)


def _dot_nt(a, b):
    return lax.dot_general(a, b, (((1,), (1,)), ((), ())), preferred_element_type=F32)


def _dot_tn(a, b):
    return lax.dot_general(a, b, (((0,), (0,)), ((), ())), preferred_element_type=F32)


def _sigmoid(x):
    return 0.5 * jnp.tanh(0.5 * x) + 0.5


def _silu(x):
    return x * _sigmoid(x)


def _layer_norm(y, g, b):
    mu = jnp.mean(y, axis=-1, keepdims=True)
    d = y - mu
    var = jnp.mean(d * d, axis=-1, keepdims=True)
    return d * lax.rsqrt(var + LN_EPS) * g + b


def _pick_tile(n, target, quantum):
    if n <= target:
        return n
    t = (target // quantum) * quantum
    while t > quantum and n % t:
        t -= quantum
    assert n % t == 0, (n, target, quantum)
    return t


def _ffn_ln_kernel(x_ref, xs_ref, g_ref, b_ref, *rest, alpha, n_main, has_tail, emit_bf16):
    main_w, rest = rest[:3], rest[3:]
    tail_w, rest = (rest[:3], rest[3:]) if has_tail else (None, rest)
    if emit_bf16:
        o_ref, os_ref, ob_ref, osb_ref, xb_scr = rest
    else:
        (o_ref, os_ref, xb_scr), ob_ref, osb_ref = rest, None, None
    tm = x_ref.shape[0]
    j = pl.program_id(1)
    n_steps = n_main + int(has_tail)

    @pl.when(j == 0)
    def _():
        for src, acc_ref, rows in ((x_ref, o_ref, slice(0, tm)), (xs_ref, os_ref, slice(tm, None))):
            x = src[...]
            xb_scr[rows] = x.astype(BF16)
            acc_ref[...] = (2.0 * alpha) * x

    def accumulate(wg_ref, wu_ref, wd_ref):
        xb = xb_scr[...]
        gate = _dot(xb, wg_ref[...].astype(BF16))
        up = _dot(xb, wu_ref[...].astype(BF16))
        h = (_silu(gate) * up).astype(BF16)
        part = _dot(h, wd_ref[...].astype(BF16))
        o_ref[...] += part[:tm]
        os_ref[...] += part[tm:]

    if has_tail:
        pl.when(j < n_main)(functools.partial(accumulate, *main_w))
        pl.when(j == n_main)(functools.partial(accumulate, *tail_w))
    else:
        accumulate(*main_w)

    @pl.when(j == n_steps - 1)
    def _():
        for acc_ref, bf_ref in ((o_ref, ob_ref), (os_ref, osb_ref)):
            y = _layer_norm(0.5 * acc_ref[...], g_ref[...], b_ref[...])
            acc_ref[...] = y
            if emit_bf16:
                bf_ref[...] = y.astype(BF16)


def _ffn_ln(x, xs, wg, wu, wd, ln_g, ln_b, *, tm, alpha, emit_bf16, name):
    rows, d = x.shape
    f = wg.shape[1]
    n_tiles = rows // tm
    ts = xs.shape[0] // n_tiles
    assert tm * n_tiles == rows and ts * n_tiles == xs.shape[0] and ts % BF16_SUBLANES == 0
    tf = min(FFN_HIDDEN_TILE, f)
    n_main = f // tf
    tail = f - n_main * tf
    n_steps = n_main + int(tail > 0)
    once = dict(pipeline_mode=pl.Buffered(1))
    vec = pl.BlockSpec((1, d), lambda i, j: (0, 0))
    main_j = lambda j: jnp.minimum(j, n_main - 1)
    in_specs = [
        pl.BlockSpec((tm, d), lambda i, j: (i, 0)), pl.BlockSpec((ts, d), lambda i, j: (i, 0)), vec, vec,
        pl.BlockSpec((d, tf), lambda i, j: (0, main_j(j))),
        pl.BlockSpec((d, tf), lambda i, j: (0, main_j(j))),
        pl.BlockSpec((tf, d), lambda i, j: (main_j(j), 0)),
    ]
    operands = [x, xs, ln_g, ln_b, wg, wu, wd]
    if tail:
        in_specs += [pl.BlockSpec((d, tail), lambda i, j: (0, 0), **once),
                     pl.BlockSpec((d, tail), lambda i, j: (0, 0), **once),
                     pl.BlockSpec((tail, d), lambda i, j: (0, 0), **once)]
        operands += [wg[:, n_main * tf:], wu[:, n_main * tf:], wd[n_main * tf:, :]]
    dtypes = (F32, BF16) if emit_bf16 else (F32,)
    out_shape, out_specs = [], []
    for dt in dtypes:
        out_shape += [jax.ShapeDtypeStruct((rows, d), dt), jax.ShapeDtypeStruct(xs.shape, dt)]
        out_specs += [pl.BlockSpec((tm, d), lambda i, j: (i, 0), **once),
                      pl.BlockSpec((ts, d), lambda i, j: (i, 0), **once)]
    return pl.pallas_call(
        functools.partial(_ffn_ln_kernel, alpha=alpha, n_main=n_main, has_tail=tail > 0, emit_bf16=emit_bf16),
        grid=(n_tiles, n_steps),
        in_specs=in_specs,
        out_specs=out_specs,
        out_shape=out_shape,
        scratch_shapes=[pltpu.VMEM((tm + ts, d), BF16)],
        compiler_params=_params("arbitrary", "arbitrary"),
        name=name,
    )(*operands)


def _apply_epilogue(kind, acc):
    if kind == "silu":
        return _silu(acc)
    if kind == "sigmoid":
        return _sigmoid(acc)
    assert kind == "none", kind
    return acc


def _proj_kernel(x_ref, xs_ref, w_ref, *rest, epilogues, tiles_per_group, dk, k_scale):
    o_ref, os_ref, wb_scr = rest[-3:]
    tm = x_ref.shape[0]

    @pl.when(pl.program_id(1) == 0)
    def _():
        wb_scr[...] = w_ref[...].astype(BF16)

    acc = _dot(jnp.concatenate([x_ref[...], xs_ref[...]], axis=0), wb_scr[...])
    group = pl.program_id(0) // tiles_per_group
    if epilogues == ("rot_q", "rot_k"):
        cos = jnp.concatenate([rest[0][...], rest[2][...]], axis=0)
        sin = jnp.concatenate([rest[1][...], rest[3][...]], axis=0)
        scale = jnp.where(group == 0, 1.0, k_scale).astype(F32)
        even = (lax.broadcasted_iota(jnp.int32, cos.shape, 1) & 1) == 0
        for hh in range(acc.shape[1] // dk):
            cols = slice(hh * dk, (hh + 1) * dk)
            a = acc[:, cols]
            swapped = jnp.where(even, pltpu.roll(a, dk - 1, 1), pltpu.roll(a, 1, 1))
            res = (a * cos + swapped * sin) * scale
            o_ref[:, cols] = res[:tm].astype(o_ref.dtype)
            os_ref[:, cols] = res[tm:].astype(os_ref.dtype)
        return
    if epilogues == ("fgate",):
        lb = rest[0][...]
        res = lb + (1.0 - lb) * _sigmoid(acc)
    else:
        kinds = sorted(set(epilogues))
        res = _apply_epilogue(kinds[0], acc)
        for kind in kinds[1:]:
            hit = functools.reduce(jnp.logical_or, [group == g for g, e in enumerate(epilogues) if e == kind])
            res = jnp.where(hit, _apply_epilogue(kind, acc), res)
    o_ref[...] = res[:tm].astype(o_ref.dtype)
    os_ref[...] = res[tm:].astype(os_ref.dtype)


def _proj(xb, xsb, w, col0, gw, *, tm, epilogues, out_dtype, name, extras=(), dk=0, k_scale=1.0):
    rows, d = xb.shape
    n_tiles = rows // tm
    ts = xsb.shape[0] // n_tiles
    assert tm * n_tiles == rows and ts * n_tiles == xsb.shape[0]
    ncols = gw * len(epilogues)
    tn = _pick_tile(gw, 1024, max(LANES, dk))
    assert col0 % tn == 0
    in_specs = [
        pl.BlockSpec((tm, d), lambda n, m: (m, 0)),
        pl.BlockSpec((ts, d), lambda n, m: (m, 0)),
        pl.BlockSpec((d, tn), lambda n, m: (0, col0 // tn + n)),
    ]
    if epilogues[0].startswith("rot"):
        tab_blocks = extras[0].shape[0] // tm
        assert tab_blocks * tm == extras[0].shape[0]
        in_specs += [pl.BlockSpec((tm, dk), lambda n, m: (m % tab_blocks, 0))] * 2
        in_specs += [pl.BlockSpec((ts, dk), lambda n, m: (m, 0))] * 2
    elif epilogues == ("fgate",):
        in_specs += [pl.BlockSpec((1, tn), lambda n, m: (0, n))]
    return pl.pallas_call(
        functools.partial(_proj_kernel, epilogues=epilogues, tiles_per_group=gw // tn, dk=dk, k_scale=k_scale),
        grid=(ncols // tn, n_tiles),
        in_specs=in_specs,
        out_specs=[pl.BlockSpec((tm, tn), lambda n, m: (m, n)), pl.BlockSpec((ts, tn), lambda n, m: (m, n))],
        out_shape=[jax.ShapeDtypeStruct((rows, ncols), out_dtype),
                   jax.ShapeDtypeStruct((xsb.shape[0], ncols), F32)],
        scratch_shapes=[pltpu.VMEM((d, tn), BF16)],
        compiler_params=_params("arbitrary", "arbitrary"),
        name=name,
    )(xb, xsb, w, *extras)


def _project_all(xb, xsb, w_in, d, tm, tables, lb, dk_ret):
    qk = _proj(xb, xsb, w_in, 0, d, tm=tm, epilogues=("rot_q", "rot_k"), out_dtype=BF16, extras=tables,
               dk=dk_ret, k_scale=dk_ret ** -0.5, name="proj_qk")
    vgq = _proj(xb, xsb, w_in, 2 * d, d, tm=tm, epilogues=("none", "silu", "silu"), out_dtype=BF16,
                name="proj_vgq")
    fg = _proj(xb, xsb, w_in, 5 * d, d, tm=tm, epilogues=("fgate",), out_dtype=F32, extras=(lb,), name="proj_f")
    rest = _proj(xb, xsb, w_in, 6 * d, d, tm=tm, epilogues=("none", "silu", "sigmoid", "sigmoid"),
                 out_dtype=BF16, name="proj_rest")
    return qk, vgq, fg, rest


def _ret_log_gamma(h):
    return math.log(1.0 - 2.0 ** (-5.0 - h))


def _ret_scan_kernel(q_ref, k_ref, v_ref, g_ref, a_ref, km_ref, vm_ref, y_ref, s_ref, s_scr, *,
                     n_heads, dk, dv, chunk, n_meta, n_chunks):
    c = pl.program_id(1)

    @pl.when(c == 0)
    def _():
        mpos = lax.broadcasted_iota(jnp.int32, (n_meta, dk), 0).astype(F32)
        for h in range(n_heads):
            lg = _ret_log_gamma(h)
            kd = (km_ref[:, h * dk:(h + 1) * dk].astype(F32) * jnp.exp(lg * (n_meta - 1.0 - mpos))).astype(BF16)
            s_scr[h] = _dot_tn(kd, vm_ref[:, h * dv:(h + 1) * dv].astype(BF16))

    rel = (lax.broadcasted_iota(jnp.int32, (chunk, chunk), 0)
           - lax.broadcasted_iota(jnp.int32, (chunk, chunk), 1)).astype(F32)
    pos_k = lax.broadcasted_iota(jnp.int32, (chunk, dk), 0).astype(F32)
    pos_v = lax.broadcasted_iota(jnp.int32, (chunk, dv), 0).astype(F32)
    for h in range(n_heads):
        lg = _ret_log_gamma(h)
        ks = slice(h * dk, (h + 1) * dk)
        vs = slice(h * dv, (h + 1) * dv)
        q = q_ref[:, ks]
        k = k_ref[:, ks]
        v = v_ref[:, vs]
        s = s_scr[h]
        decay = jnp.where(rel >= 0, jnp.exp(lg * jnp.maximum(rel, 0.0)), 0.0)
        inner = (_dot_nt(q, k) * decay).astype(BF16)
        o = _dot(inner, v) + _dot(q, s.astype(BF16)) * jnp.exp(lg * (pos_v + 1.0))
        kd = (k.astype(F32) * jnp.exp(lg * (chunk - 1.0 - pos_k))).astype(BF16)
        s_scr[h] = math.exp(lg * chunk) * s + _dot_tn(kd, v)
        mu = jnp.mean(o, axis=-1, keepdims=True)
        dlt = o - mu
        var = jnp.mean(dlt * dlt, axis=-1, keepdims=True)
        gates = g_ref[:, vs].astype(F32) * a_ref[:, vs].astype(F32)
        y_ref[:, vs] = (dlt * lax.rsqrt(var + LN_EPS) * gates).astype(y_ref.dtype)

    @pl.when(c == n_chunks - 1)
    def _():
        s_ref[0, 0] = s_scr[...]


def _ret_scan(qk, vgq, rest, qk_s, vgq_s, *, batch, seq, n_heads, dk, dv, n_meta, meta_row0, depth):
    d = n_heads * dk
    chunk = _pick_tile(seq, RET_CHUNK, BF16_SUBLANES)
    n_chunks = seq // chunk
    assert meta_row0 % n_meta == 0
    mblk = meta_row0 // n_meta
    row = lambda b, c: b * n_chunks + c
    return pl.pallas_call(
        functools.partial(_ret_scan_kernel, n_heads=n_heads, dk=dk, dv=dv, chunk=chunk,
                          n_meta=n_meta, n_chunks=n_chunks),
        grid=(batch, n_chunks),
        in_specs=[
            pl.BlockSpec((chunk, d), lambda b, c: (row(b, c), 0)),
            pl.BlockSpec((chunk, d), lambda b, c: (row(b, c), 1)),
            pl.BlockSpec((chunk, d), lambda b, c: (row(b, c), 0)),
            pl.BlockSpec((chunk, d), lambda b, c: (row(b, c), 1)),
            pl.BlockSpec((chunk, d), lambda b, c: (row(b, c), 2)),
            pl.BlockSpec((n_meta, d), lambda b, c: (mblk, 1)),
            pl.BlockSpec((n_meta, d), lambda b, c: (mblk, 0)),
        ],
        out_specs=[
            pl.BlockSpec((chunk, d), lambda b, c: (row(b, c), 0)),
            pl.BlockSpec((1, 1, n_heads, dk, dv), lambda b, c: (0, b, 0, 0, 0)),
        ],
        out_shape=[
            jax.ShapeDtypeStruct((batch * seq, d), BF16),
            jax.ShapeDtypeStruct((depth, batch, n_heads, dk, dv), F32),
        ],
        scratch_shapes=[pltpu.VMEM((n_heads, dk, dv), F32)],
        compiler_params=_params("arbitrary", "arbitrary"),
        name="ret_scan",
    )(qk, qk, vgq, vgq, rest, qk_s, vgq_s)


def _split_dot(m01, x):
    hi = x.astype(BF16)
    lo = (x - hi.astype(F32)).astype(BF16)
    return _dot(m01, hi) + _dot(m01, lo)


def _hgrn_scan_kernel(q_ref, f_ref, v_ref, g_ref, a_ref, fm_ref, vm_ref, gn_ref, *rest,
                      n_heads, dk, dv, chunk, n_meta, n_chunks, n_levels, n_small, decode_per_step, decode_dims):
    if decode_per_step:
        dec_in, rest = rest[:6], rest[6:]
        (y_ref, s_ref), dec_out, rest = rest[:2], rest[2:5], rest[5:]
    else:
        (y_ref, s_ref), rest = rest[:2], rest[2:]
    st_scr, mask_scr, sums_scr, logsum_scr = rest
    bi = pl.program_id(0)
    c = pl.program_id(1)
    row = lax.broadcasted_iota(jnp.int32, (chunk, chunk), 0)
    col = lax.broadcasted_iota(jnp.int32, (chunk, chunk), 1)

    @pl.when((bi == 0) & (c == 0))
    def _():
        mask_scr[0] = jnp.where(row == col, 1.0, 0.0).astype(BF16)
        sums_scr[0:chunk] = jnp.where(row >= col, 1.0, 0.0).astype(BF16)
        for l in range(1, n_levels + 1):
            same_block = (row >> l) == (col >> l)
            pair = (((row >> (l - 1)) & 1) == 1) & (((col >> (l - 1)) & 1) == 0)
            mask_scr[l] = jnp.where(same_block & pair, 1.0, 0.0).astype(BF16)
            if l <= n_small:
                bnd = ((row >> l) << l) + ((1 << (l - 1)) - 1)
                second = ((row >> (l - 1)) & 1) == 1
                between = (second & (col > bnd) & (col <= row)) | (~second & (col > row) & (col <= bnd))
                sums_scr[l * chunk:(l + 1) * chunk] = jnp.where(between, 1.0, 0.0).astype(BF16)

    @pl.when(c == 0)
    def _():
        mr = lax.broadcasted_iota(jnp.int32, (n_meta, n_meta), 0)
        mc = lax.broadcasted_iota(jnp.int32, (n_meta, n_meta), 1)
        later = jnp.where(mc > mr, 1.0, 0.0).astype(BF16)

        def meta_body(h, carry):
            ks = pl.ds(pl.multiple_of(h * dk, dk), dk)
            vs = pl.ds(pl.multiple_of(h * dv, dv), dv)
            f = fm_ref[:, ks]
            tail = _split_dot(later, jnp.log(f))
            kd = ((1.0 - f) * jnp.exp(tail)).astype(BF16)
            st_scr[h] = _dot_tn(vm_ref[:, vs].astype(BF16), kd)
            return carry

        lax.fori_loop(0, n_heads, meta_body, 0)

    logf = jnp.log(f_ref[...])
    logf_hi = logf.astype(BF16)
    logf_lo = (logf - logf_hi.astype(F32)).astype(BF16)
    logsum_scr[0:chunk] = _dot(sums_scr[0:chunk], logf_hi) + _dot(sums_scr[0:chunk], logf_lo)
    if n_small:
        logsum_scr[chunk:] = _dot(sums_scr[chunk:], logf_hi)

    def head_body(h):
        ks = pl.ds(pl.multiple_of(h * dk, dk), dk)
        vs = pl.ds(pl.multiple_of(h * dv, dv), dv)
        k = 1.0 - f_ref[:, ks]
        kb = k.astype(BF16)
        b = logsum_scr[0:chunk, ks]
        qb = q_ref[:, ks]
        q = qb.astype(F32)
        v = v_ref[:, vs]
        att = mask_scr[0] * _dot_nt(qb, kb).astype(BF16)
        for l in range(1, n_levels + 1):
            m = 1 << l
            half = m >> 1
            if l <= n_small:
                e = jnp.exp(logsum_scr[l * chunk:(l + 1) * chunk, ks]).astype(BF16)
                prod = _dot_nt(qb * e, kb * e)
            else:
                q_rows, k_rows = [], []
                zero = jnp.zeros((half, dk), F32)
                for p in range(chunk // m):
                    r = b[p * m + half - 1:p * m + half, :]
                    lo = slice(p * m, p * m + half)
                    hi = slice(p * m + half, (p + 1) * m)
                    k_rows += [k[lo] * jnp.exp(r - b[lo]), zero]
                    q_rows += [zero, q[hi] * jnp.exp(b[hi] - r)]
                prod = _dot_nt(jnp.concatenate(q_rows, axis=0).astype(BF16),
                               jnp.concatenate(k_rows, axis=0).astype(BF16))
            prod = prod.astype(BF16)
            att = att + (prod if m == chunk else mask_scr[l] * prod)
        st = st_scr[h]
        o = _dot(att, v) + _dot_nt((q * jnp.exp(b)).astype(BF16), st.astype(BF16))
        btot = b[chunk - 1:chunk, :]
        kd = (k * jnp.exp(btot - b)).astype(BF16)
        st_scr[h] = jnp.exp(btot) * st + _dot_tn(v, kd)
        ms = jnp.mean(o * o, axis=-1, keepdims=True)
        gates = g_ref[:, vs].astype(F32) * a_ref[:, vs].astype(F32)
        y_ref[:, vs] = (o * lax.rsqrt(ms + LN_EPS) * gn_ref[:, vs] * gates).astype(y_ref.dtype)

    n_trips = decode_per_step or n_heads // HGRN_HEAD_UNROLL
    heads_per_trip = n_heads // n_trips

    def trip(t, carry):
        for hh in range(heads_per_trip):
            head_body(t * heads_per_trip + hh)
        if decode_per_step:
            _decode_one(t, *dec_in[:4], gn_ref, *dec_in[4:], *dec_out, **decode_dims)
        return carry

    lax.fori_loop(0, n_trips, trip, 0)

    @pl.when(c == n_chunks - 1)
    def _():
        for h in range(n_heads):
            s_ref[0, 0, h] = st_scr[h].T


def _decode_fits_scan(n_dec, batch, seq, n_heads):
    steps = batch * (seq // _pick_tile(seq, HGRN_CHUNK, BF16_SUBLANES))
    per = n_dec // steps
    return per if per and per * steps == n_dec and n_heads % per == 0 else 0


def _hgrn_scan(vgq, fg, rest, fg_s, rest_s, gn, *, batch, seq, n_heads, dk, dv, n_meta, meta_row0, depth,
               decode=None):
    d = n_heads * dk
    chunk = _pick_tile(seq, HGRN_CHUNK, BF16_SUBLANES)
    assert chunk & (chunk - 1) == 0 and chunk >= 8
    n_levels = chunk.bit_length() - 1
    n_small = min(n_levels, F32_SUBLANES.bit_length() - 1)
    n_chunks = seq // chunk
    mblk = meta_row0 // n_meta
    row = lambda b, c: b * n_chunks + c
    in_specs = [
        pl.BlockSpec((chunk, d), lambda b, c: (row(b, c), 2)),
        pl.BlockSpec((chunk, d), lambda b, c: (row(b, c), 0)),
        pl.BlockSpec((chunk, d), lambda b, c: (row(b, c), 0)),
        pl.BlockSpec((chunk, d), lambda b, c: (row(b, c), 1)),
        pl.BlockSpec((chunk, d), lambda b, c: (row(b, c), 3)),
        pl.BlockSpec((n_meta, d), lambda b, c: (mblk, 0)),
        pl.BlockSpec((n_meta, d), lambda b, c: (mblk, 0)),
        pl.BlockSpec((1, d), lambda b, c: (0, 0)),
    ]
    operands = [vgq, fg, rest, rest, rest, fg_s, rest_s, gn]
    out_specs = [
        pl.BlockSpec((chunk, d), lambda b, c: (row(b, c), 0)),
        pl.BlockSpec((1, 1, n_heads, dk, dv), lambda b, c: (0, b, 0, 0, 0)),
    ]
    out_shape = [
        jax.ShapeDtypeStruct((batch * seq, d), BF16),
        jax.ShapeDtypeStruct((depth, batch, n_heads, dk, dv), F32),
    ]
    per, decode_dims = 0, None
    if decode is not None:
        qk_s, vgq_s, state_ret, state_hgrn, per = decode
        _, n_dec, h_ret, dk_r, dv_r = state_ret.shape
        decode_dims = dict(d=d, h_ret=h_ret, dk_r=dk_r, dv_r=dv_r, h_hg=n_heads, dk_h=dk, dv_h=dv)
        as3d = lambda a: a.reshape(a.shape[0], 1, a.shape[1])
        row_spec = lambda w: pl.BlockSpec((per, 1, w), lambda b, c: (row(b, c), 0, 0))
        ret_spec = pl.BlockSpec((1, per, h_ret, dk_r, dv_r), lambda b, c: (0, row(b, c), 0, 0, 0))
        hg_spec = pl.BlockSpec((1, per, n_heads, dk, dv), lambda b, c: (0, row(b, c), 0, 0, 0))
        in_specs += [row_spec(2 * d), row_spec(3 * d), row_spec(d), row_spec(4 * d), ret_spec, hg_spec]
        operands += [as3d(qk_s), as3d(vgq_s), as3d(fg_s), as3d(rest_s), state_ret, state_hgrn]
        out_specs += [row_spec(d), ret_spec, hg_spec]
        out_shape += [jax.ShapeDtypeStruct((n_dec, 1, d), F32), jax.ShapeDtypeStruct(state_ret.shape, F32),
                      jax.ShapeDtypeStruct(state_hgrn.shape, F32)]
    return pl.pallas_call(
        functools.partial(_hgrn_scan_kernel, n_heads=n_heads, dk=dk, dv=dv, chunk=chunk,
                          n_meta=n_meta, n_chunks=n_chunks, n_levels=n_levels, n_small=n_small,
                          decode_per_step=per, decode_dims=decode_dims),
        grid=(batch, n_chunks),
        in_specs=in_specs,
        out_specs=out_specs,
        out_shape=out_shape,
        scratch_shapes=[
            pltpu.VMEM((n_heads, dv, dk), F32),
            pltpu.VMEM((n_levels + 1, chunk, chunk), BF16),
            pltpu.VMEM(((n_small + 1) * chunk, chunk), BF16),
            pltpu.VMEM(((n_small + 1) * chunk, d), F32),
        ],
        compiler_params=_params("arbitrary", "arbitrary"),
        name="hgrn_scan",
    )(*operands)


def _column_bcast(r, n_rows, n_cols):
    return jnp.broadcast_to(r, (n_cols, n_rows)).T


def _decode_kernel(qk_ref, vgq_ref, f_ref, rest_ref, gn_ref, sr_ref, sh_ref,
                   y_ref, sro_ref, sho_ref, *, per_step, **dims):
    for e in range(per_step):
        _decode_one(e, qk_ref, vgq_ref, f_ref, rest_ref, gn_ref, sr_ref, sh_ref, y_ref, sro_ref, sho_ref, **dims)


def _decode_one(e, qk_ref, vgq_ref, f_ref, rest_ref, gn_ref, sr_ref, sh_ref,
                y_ref, sro_ref, sho_ref, *, d, h_ret, dk_r, dv_r, h_hg, dk_h, dv_h):
    qk = qk_ref[e]
    vgq = vgq_ref[e]
    fg = f_ref[e]
    rest = rest_ref[e]
    gn = gn_ref[...]
    out_r = []
    for h in range(h_ret):
        q = qk[:, h * dk_r:(h + 1) * dk_r]
        k = qk[:, d + h * dk_r:d + (h + 1) * dk_r]
        v = vgq[:, h * dv_r:(h + 1) * dv_r]
        s_new = math.exp(_ret_log_gamma(h)) * sr_ref[0, e, h] + _column_bcast(k, dk_r, dv_r) * v
        sro_ref[0, e, h] = s_new
        o = _dot(jnp.broadcast_to(q, (8, dk_r)).astype(BF16), s_new.astype(BF16))[0:1, :]
        mu = jnp.mean(o, axis=-1, keepdims=True)
        dlt = o - mu
        var = jnp.mean(dlt * dlt, axis=-1, keepdims=True)
        gates = vgq[:, d + h * dv_r:d + (h + 1) * dv_r] * rest[:, 2 * d + h * dv_r:2 * d + (h + 1) * dv_r]
        out_r.append(dlt * lax.rsqrt(var + LN_EPS) * gates)
    out_h = []
    for h in range(h_hg):
        ks = slice(h * dk_h, (h + 1) * dk_h)
        vs = slice(h * dv_h, (h + 1) * dv_h)
        f = fg[:, ks]
        q = vgq[:, 2 * d + h * dk_h:2 * d + (h + 1) * dk_h]
        v = rest[:, vs]
        s_new = _column_bcast(f, dk_h, dv_h) * sh_ref[0, e, h] + _column_bcast(1.0 - f, dk_h, dv_h) * v
        sho_ref[0, e, h] = s_new
        o = _dot(jnp.broadcast_to(q, (8, dk_h)).astype(BF16), s_new.astype(BF16))[0:1, :]
        ms = jnp.mean(o * o, axis=-1, keepdims=True)
        gates = rest[:, d + h * dv_h:d + (h + 1) * dv_h] * rest[:, 3 * d + h * dv_h:3 * d + (h + 1) * dv_h]
        out_h.append(o * lax.rsqrt(ms + LN_EPS) * gn[:, vs] * gates)
    y_ref[e] = jnp.concatenate(out_r, axis=1) + jnp.concatenate(out_h, axis=1)


def _decode(qk_s, vgq_s, fg_s, rest_s, gn, state_ret, state_hgrn, *, n_dec):
    depth, _, h_ret, dk_r, dv_r = state_ret.shape
    _, _, h_hg, dk_h, dv_h = state_hgrn.shape
    d = h_ret * dk_r
    rows = qk_s.shape[0]
    per = DECODE_ROWS_PER_STEP if n_dec % DECODE_ROWS_PER_STEP == 0 else 1
    as3d = lambda a: a.reshape(rows, 1, a.shape[1])
    row_spec = lambda w: pl.BlockSpec((per, 1, w), lambda b: (b, 0, 0))
    ret_spec = pl.BlockSpec((1, per, h_ret, dk_r, dv_r), lambda b: (0, b, 0, 0, 0))
    hg_spec = pl.BlockSpec((1, per, h_hg, dk_h, dv_h), lambda b: (0, b, 0, 0, 0))
    y, sr, sh = pl.pallas_call(
        functools.partial(_decode_kernel, per_step=per, d=d, h_ret=h_ret, dk_r=dk_r, dv_r=dv_r,
                          h_hg=h_hg, dk_h=dk_h, dv_h=dv_h),
        grid=(n_dec // per,),
        in_specs=[row_spec(2 * d), row_spec(3 * d), row_spec(d), row_spec(4 * d),
                  pl.BlockSpec((1, d), lambda b: (0, 0)), ret_spec, hg_spec],
        out_specs=[row_spec(d), ret_spec, hg_spec],
        out_shape=[
            jax.ShapeDtypeStruct((n_dec, 1, d), F32),
            jax.ShapeDtypeStruct(state_ret.shape, F32),
            jax.ShapeDtypeStruct(state_hgrn.shape, F32),
        ],
        compiler_params=_params("arbitrary"),
        name="decode",
    )(as3d(qk_s), as3d(vgq_s), as3d(fg_s), as3d(rest_s), gn, state_ret, state_hgrn)
    return y.reshape(n_dec, d), sr, sh


def _wout_ln_kernel(*refs, alpha, n_y):
    y_refs = refs[:n_y]
    x_ref, w_ref, g_ref, b_ref, o_ref = refs[n_y:]
    y = y_refs[0][...].astype(F32)
    for r in y_refs[1:]:
        y = y + r[...].astype(F32)
    m = _dot(y.astype(BF16), w_ref[...])
    o_ref[...] = _layer_norm(alpha * x_ref[...] + m, g_ref[...], b_ref[...])


def _wout_ln(ys, x, w, ln_g, ln_b, *, alpha, name):
    rows, d = x.shape
    tm = _pick_tile(rows, 512, BF16_SUBLANES)
    tile = pl.BlockSpec((tm, d), lambda i: (i, 0))
    vec = pl.BlockSpec((1, d), lambda i: (0, 0))
    return pl.pallas_call(
        functools.partial(_wout_ln_kernel, alpha=alpha, n_y=len(ys)),
        grid=(rows // tm,),
        in_specs=[tile] * len(ys) + [tile, pl.BlockSpec((d, d), lambda i: (0, 0)), vec, vec],
        out_specs=tile,
        out_shape=jax.ShapeDtypeStruct((rows, d), F32),
        compiler_params=_params("arbitrary"),
        name=name,
    )(*ys, x, w, ln_g, ln_b)


def _rotary_tables(pos, dk):
    inv = ROPE_BASE ** (-jnp.arange(0, dk, 2, dtype=F32) / dk)
    ang = pos.astype(F32)[:, None] * inv[None, :]
    cos, sin = jnp.cos(ang), jnp.sin(ang)
    cos_full = jnp.stack([cos, cos], axis=-1).reshape(pos.shape[0], dk)
    sin_signed = jnp.stack([-sin, sin], axis=-1).reshape(pos.shape[0], dk)
    return cos_full, sin_signed


def kernel(x_prompt, x_sample, state_ret, state_hgrn, meta_tokens, ln1_g, ln1_b, ffn1_w_gate, ffn1_w_up, ffn1_w_down, w_in, hgrn_lb_logits, hgrn_norm_g, w_out, ln2_g, ln2_b, ffn2_w_gate, ffn2_w_up, ffn2_w_down, ln3_g, ln3_b):
    batch, seq, d = x_prompt.shape
    n_dec, dec_seq, _ = x_sample.shape
    depth, _, h_ret, dk_r, dv_r = state_ret.shape
    _, _, h_hg, dk_h, dv_h = state_hgrn.shape
    n_meta = meta_tokens.shape[0]
    assert depth == 1 and dec_seq == 1
    alpha = (2.0 * depth) ** 0.25
    layer = 0

    ffn1 = (ffn1_w_gate[layer], ffn1_w_up[layer], ffn1_w_down[layer])
    ffn2 = (ffn2_w_gate[layer], ffn2_w_up[layer], ffn2_w_down[layer])
    w_in_b = w_in[layer]
    w_out_b = w_out[layer].astype(BF16)
    vec = lambda a: a[layer].reshape(1, d)
    lb = jnp.cumsum(jax.nn.softmax(hgrn_lb_logits.astype(F32), axis=0), axis=0)[layer].reshape(1, d)
    gn = vec(hgrn_norm_g)

    tm = _pick_tile(seq, ROW_TILE, BF16_SUBLANES)
    n_tiles = batch * seq // tm
    small_quantum = BF16_SUBLANES * n_tiles
    n_small = -(-(n_dec + n_meta) // small_quantum) * small_quantum
    n_pad = n_small - n_dec - n_meta
    xp = x_prompt.reshape(batch * seq, d)
    xs = jnp.concatenate([x_sample.reshape(n_dec, d), meta_tokens.astype(x_prompt.dtype),
                          jnp.zeros((n_pad, d), x_prompt.dtype)], axis=0)
    cos_p, sin_p = _rotary_tables(n_meta + jnp.arange(seq, dtype=jnp.int32), dk_r)
    pos_s = jnp.concatenate([jnp.full((n_dec,), PAST_LEN, jnp.int32), jnp.arange(n_meta, dtype=jnp.int32),
                             jnp.zeros((n_pad,), jnp.int32)])
    cos_s, sin_s = _rotary_tables(pos_s, dk_r)

    x1p, x1s, x1pb, x1sb = _ffn_ln(xp, xs, *ffn1, vec(ln1_g), vec(ln1_b), tm=tm, alpha=alpha, emit_bf16=True,
                                   name="ffn1")
    (qk_p, qk_s), (vgq_p, vgq_s), (fg_p, fg_s), (rest_p, rest_s) = _project_all(
        x1pb, x1sb, w_in_b, d, tm, (cos_p, sin_p, cos_s, sin_s), lb, dk_r)

    yr_p, state_ret_prompt = _ret_scan(qk_p, vgq_p, rest_p, qk_s, vgq_s, batch=batch, seq=seq, n_heads=h_ret,
                                       dk=dk_r, dv=dv_r, n_meta=n_meta, meta_row0=n_dec, depth=depth)
    hgrn_args = dict(batch=batch, seq=seq, n_heads=h_hg, dk=dk_h, dv=dv_h, n_meta=n_meta, meta_row0=n_dec,
                     depth=depth)
    per = _decode_fits_scan(n_dec, batch, seq, h_hg)
    if per:
        yh_p, state_hgrn_prompt, y_s, state_ret_sample, state_hgrn_sample = _hgrn_scan(
            vgq_p, fg_p, rest_p, fg_s, rest_s, gn, decode=(qk_s, vgq_s, state_ret, state_hgrn, per), **hgrn_args)
        y_s = y_s.reshape(n_dec, d)
    else:
        yh_p, state_hgrn_prompt = _hgrn_scan(vgq_p, fg_p, rest_p, fg_s, rest_s, gn, **hgrn_args)
        y_s, state_ret_sample, state_hgrn_sample = _decode(qk_s, vgq_s, fg_s, rest_s, gn, state_ret,
                                                           state_hgrn, n_dec=n_dec)

    x2p = _wout_ln([yr_p, yh_p], x1p, w_out_b, vec(ln2_g), vec(ln2_b), alpha=alpha, name="wout_prompt")
    x2s = _wout_ln([y_s], x1s[:n_dec], w_out_b, vec(ln2_g), vec(ln2_b), alpha=alpha, name="wout_small")

    x2s = jnp.pad(x2s, ((0, n_small - n_dec), (0, 0)))
    y_prompt, y_small = _ffn_ln(x2p, x2s, *ffn2, vec(ln3_g), vec(ln3_b), tm=tm, alpha=alpha, emit_bf16=False,
                                name="ffn2")
    y_sample = y_small[:n_dec]

    return (y_prompt.reshape(batch, seq, d), y_sample.reshape(n_dec, dec_seq, d), state_ret_prompt,
            state_ret_sample, state_hgrn_prompt, state_hgrn_sample)
```

```python
import functools
import math

import jax
import jax.numpy as jnp
from jax import lax
from jax.experimental import pallas as pl
from jax.experimental.pallas import tpu as pltpu

F32 = jnp.float32
BF16 = jnp.bfloat16

PAST_LEN = 16384
LN_EPS = 1e-5
ROPE_BASE = 10000.0

V7X_VMEM_BYTES = 64 * 1024 * 1024
VMEM_LIMIT_BYTES = V7X_VMEM_BYTES - 8 * 1024 * 1024
LANES = 128
F32_SUBLANES = 8
BF16_SUBLANES = 16

ROW_TILE = 1024
FFN_HIDDEN_TILE = 512
RET_CHUNK = 256
HGRN_CHUNK = 128
DECODE_ROWS_PER_STEP = 2
HGRN_HEAD_UNROLL = 8


def _params(*sem):
    return pltpu.CompilerParams(dimension_semantics=sem, vmem_limit_bytes=VMEM_LIMIT_BYTES)


def _dot(a, b):
    return jnp.dot(a, b, preferred_element_type=F32)


def _dot_nt(a, b):
    return lax.dot_general(a, b, (((1,), (1,)), ((), ())), preferred_element_type=F32)


def _dot_tn(a, b):
    return lax.dot_general(a, b, (((0,), (0,)), ((), ())), preferred_element_type=F32)


def _sigmoid(x):
    return 0.5 * jnp.tanh(0.5 * x) + 0.5


def _silu(x):
    return x * _sigmoid(x)


def _layer_norm(y, g, b):
    mu = jnp.mean(y, axis=-1, keepdims=True)
    d = y - mu
    var = jnp.mean(d * d, axis=-1, keepdims=True)
    return d * lax.rsqrt(var + LN_EPS) * g + b


def _pick_tile(n, target, quantum):
    if n <= target:
        return n
    t = (target // quantum) * quantum
    while t > quantum and n % t:
        t -= quantum
    assert n % t == 0, (n, target, quantum)
    return t


def _ffn_up_kernel(x_ref, xs_ref, wg_ref, wu_ref, h_ref, hs_ref, wgb_scr, wub_scr):
    tm = x_ref.shape[0]

    @pl.when(pl.program_id(1) == 0)
    def _():
        wgb_scr[...] = wg_ref[...].astype(BF16)
        wub_scr[...] = wu_ref[...].astype(BF16)

    xb = jnp.concatenate([x_ref[...], xs_ref[...]], axis=0).astype(BF16)
    h = (_silu(_dot(xb, wgb_scr[...])) * _dot(xb, wub_scr[...])).astype(BF16)
    h_ref[...] = h[:tm]
    hs_ref[...] = h[tm:]


def _ffn_up(x, xs, wg, wu, f, tf, *, tm, name):
    rows, d = x.shape
    n_tiles = rows // tm
    ts = xs.shape[0] // n_tiles
    assert f % tf == 0
    w_spec = pl.BlockSpec((d, tf), lambda j, i: (0, j))
    return pl.pallas_call(
        _ffn_up_kernel,
        grid=(f // tf, n_tiles),
        in_specs=[pl.BlockSpec((tm, d), lambda j, i: (i, 0)), pl.BlockSpec((ts, d), lambda j, i: (i, 0)),
                  w_spec, w_spec],
        out_specs=[pl.BlockSpec((tm, tf), lambda j, i: (i, j)), pl.BlockSpec((ts, tf), lambda j, i: (i, j))],
        out_shape=[jax.ShapeDtypeStruct((rows, f), BF16), jax.ShapeDtypeStruct((xs.shape[0], f), BF16)],
        scratch_shapes=[pltpu.VMEM((d, tf), BF16), pltpu.VMEM((d, tf), BF16)],
        compiler_params=_params("arbitrary", "arbitrary"),
        name=name,
    )(x, xs, wg, wu)


def _ffn_down_ln_kernel(x_ref, xs_ref, g_ref, b_ref, *rest, alpha, n_main, has_tail, emit_bf16):
    main, rest = rest[:3], rest[3:]
    tail, rest = (rest[:3], rest[3:]) if has_tail else (None, rest)
    if emit_bf16:
        o_ref, os_ref, ob_ref, osb_ref = rest
    else:
        (o_ref, os_ref), ob_ref, osb_ref = rest, None, None
    tm = x_ref.shape[0]
    k = pl.program_id(1)
    n_steps = n_main + int(has_tail)

    @pl.when(k == 0)
    def _():
        o_ref[...] = (2.0 * alpha) * x_ref[...]
        os_ref[...] = (2.0 * alpha) * xs_ref[...]

    def accumulate(h_ref, hs_ref, wd_ref):
        h = jnp.concatenate([h_ref[...], hs_ref[...]], axis=0)
        part = _dot(h, wd_ref[...].astype(BF16))
        o_ref[...] += part[:tm]
        os_ref[...] += part[tm:]

    if has_tail:
        pl.when(k < n_main)(functools.partial(accumulate, *main))
        pl.when(k == n_main)(functools.partial(accumulate, *tail))
    else:
        accumulate(*main)

    @pl.when(k == n_steps - 1)
    def _():
        for acc_ref, bf_ref in ((o_ref, ob_ref), (os_ref, osb_ref)):
            y = _layer_norm(0.5 * acc_ref[...], g_ref[...], b_ref[...])
            acc_ref[...] = y
            if emit_bf16:
                bf_ref[...] = y.astype(BF16)


def _ffn_ln(x, xs, wg, wu, wd, ln_g, ln_b, *, tm, alpha, emit_bf16, name):
    rows, d = x.shape
    f = wg.shape[1]
    n_tiles = rows // tm
    ts = xs.shape[0] // n_tiles
    assert tm * n_tiles == rows and ts * n_tiles == xs.shape[0] and ts % BF16_SUBLANES == 0
    tf = min(FFN_HIDDEN_TILE, f)
    n_main = f // tf
    f_main = n_main * tf
    tail = f - f_main
    n_steps = n_main + int(tail > 0)

    h, hs = _ffn_up(x, xs, wg, wu, f_main, tf, tm=tm, name=name + "_up")
    once = dict(pipeline_mode=pl.Buffered(1))
    vec = pl.BlockSpec((1, d), lambda i, k: (0, 0))
    main_k = lambda k: jnp.minimum(k, n_main - 1)
    in_specs = [
        pl.BlockSpec((tm, d), lambda i, k: (i, 0)), pl.BlockSpec((ts, d), lambda i, k: (i, 0)), vec, vec,
        pl.BlockSpec((tm, tf), lambda i, k: (i, main_k(k))),
        pl.BlockSpec((ts, tf), lambda i, k: (i, main_k(k))),
        pl.BlockSpec((tf, d), lambda i, k: (main_k(k), 0)),
    ]
    operands = [x, xs, ln_g, ln_b, h, hs, wd]
    if tail:
        ht, hts = _ffn_up(x, xs, wg[:, f_main:], wu[:, f_main:], tail, tail, tm=tm, name=name + "_up_tail")
        in_specs += [pl.BlockSpec((tm, tail), lambda i, k: (i, 0)), pl.BlockSpec((ts, tail), lambda i, k: (i, 0)),
                     pl.BlockSpec((tail, d), lambda i, k: (0, 0), **once)]
        operands += [ht, hts, wd[f_main:, :]]
    dtypes = (F32, BF16) if emit_bf16 else (F32,)
    out_shape, out_specs = [], []
    for dt in dtypes:
        out_shape += [jax.ShapeDtypeStruct((rows, d), dt), jax.ShapeDtypeStruct(xs.shape, dt)]
        out_specs += [pl.BlockSpec((tm, d), lambda i, k: (i, 0), **once),
                      pl.BlockSpec((ts, d), lambda i, k: (i, 0), **once)]
    return pl.pallas_call(
        functools.partial(_ffn_down_ln_kernel, alpha=alpha, n_main=n_main, has_tail=tail > 0,
                          emit_bf16=emit_bf16),
        grid=(n_tiles, n_steps),
        in_specs=in_specs,
        out_specs=out_specs,
        out_shape=out_shape,
        compiler_params=_params("arbitrary", "arbitrary"),
        name=name + "_down",
    )(*operands)


def _apply_epilogue(kind, acc):
    if kind == "silu":
        return _silu(acc)
    if kind == "sigmoid":
        return _sigmoid(acc)
    assert kind == "none", kind
    return acc


def _proj_kernel(x_ref, xs_ref, w_ref, *rest, epilogues, tiles_per_group, dk, k_scale):
    o_ref, os_ref, wb_scr = rest[-3:]
    tm = x_ref.shape[0]

    @pl.when(pl.program_id(1) == 0)
    def _():
        wb_scr[...] = w_ref[...].astype(BF16)

    acc = _dot(jnp.concatenate([x_ref[...], xs_ref[...]], axis=0), wb_scr[...])
    group = pl.program_id(0) // tiles_per_group
    if epilogues == ("rot_q", "rot_k"):
        cos = jnp.concatenate([rest[0][...], rest[2][...]], axis=0)
        sin = jnp.concatenate([rest[1][...], rest[3][...]], axis=0)
        scale = jnp.where(group == 0, 1.0, k_scale).astype(F32)
        even = (lax.broadcasted_iota(jnp.int32, cos.shape, 1) & 1) == 0
        for hh in range(acc.shape[1] // dk):
            cols = slice(hh * dk, (hh + 1) * dk)
            a = acc[:, cols]
            swapped = jnp.where(even, pltpu.roll(a, dk - 1, 1), pltpu.roll(a, 1, 1))
            res = (a * cos + swapped * sin) * scale
            o_ref[:, cols] = res[:tm].astype(o_ref.dtype)
            os_ref[:, cols] = res[tm:].astype(os_ref.dtype)
        return
    if epilogues == ("fgate",):
        lb = rest[0][...]
        res = lb + (1.0 - lb) * _sigmoid(acc)
    else:
        kinds = sorted(set(epilogues))
        res = _apply_epilogue(kinds[0], acc)
        for kind in kinds[1:]:
            hit = functools.reduce(jnp.logical_or, [group == g for g, e in enumerate(epilogues) if e == kind])
            res = jnp.where(hit, _apply_epilogue(kind, acc), res)
    o_ref[...] = res[:tm].astype(o_ref.dtype)
    os_ref[...] = res[tm:].astype(os_ref.dtype)


def _proj(xb, xsb, w, col0, gw, *, tm, epilogues, out_dtype, name, extras=(), dk=0, k_scale=1.0):
    rows, d = xb.shape
    n_tiles = rows // tm
    ts = xsb.shape[0] // n_tiles
    assert tm * n_tiles == rows and ts * n_tiles == xsb.shape[0]
    ncols = gw * len(epilogues)
    tn = _pick_tile(gw, 1024, max(LANES, dk))
    assert col0 % tn == 0
    in_specs = [
        pl.BlockSpec((tm, d), lambda n, m: (m, 0)),
        pl.BlockSpec((ts, d), lambda n, m: (m, 0)),
        pl.BlockSpec((d, tn), lambda n, m: (0, col0 // tn + n)),
    ]
    if epilogues[0].startswith("rot"):
        tab_blocks = extras[0].shape[0] // tm
        assert tab_blocks * tm == extras[0].shape[0]
        in_specs += [pl.BlockSpec((tm, dk), lambda n, m: (m % tab_blocks, 0))] * 2
        in_specs += [pl.BlockSpec((ts, dk), lambda n, m: (m, 0))] * 2
    elif epilogues == ("fgate",):
        in_specs += [pl.BlockSpec((1, tn), lambda n, m: (0, n))]
    return pl.pallas_call(
        functools.partial(_proj_kernel, epilogues=epilogues, tiles_per_group=gw // tn, dk=dk, k_scale=k_scale),
        grid=(ncols // tn, n_tiles),
        in_specs=in_specs,
        out_specs=[pl.BlockSpec((tm, tn), lambda n, m: (m, n)), pl.BlockSpec((ts, tn), lambda n, m: (m, n))],
        out_shape=[jax.ShapeDtypeStruct((rows, ncols), out_dtype),
                   jax.ShapeDtypeStruct((xsb.shape[0], ncols), F32)],
        scratch_shapes=[pltpu.VMEM((d, tn), BF16)],
        compiler_params=_params("arbitrary", "arbitrary"),
        name=name,
    )(xb, xsb, w, *extras)


def _project_all(xb, xsb, w_in, d, tm, tables, lb, dk_ret):
    qk = _proj(xb, xsb, w_in, 0, d, tm=tm, epilogues=("rot_q", "rot_k"), out_dtype=BF16, extras=tables,
               dk=dk_ret, k_scale=dk_ret ** -0.5, name="proj_qk")
    vgq = _proj(xb, xsb, w_in, 2 * d, d, tm=tm, epilogues=("none", "silu", "silu"), out_dtype=BF16,
                name="proj_vgq")
    fg = _proj(xb, xsb, w_in, 5 * d, d, tm=tm, epilogues=("fgate",), out_dtype=F32, extras=(lb,), name="proj_f")
    rest = _proj(xb, xsb, w_in, 6 * d, d, tm=tm, epilogues=("none", "silu", "sigmoid", "sigmoid"),
                 out_dtype=BF16, name="proj_rest")
    return qk, vgq, fg, rest


def _ret_log_gamma(h):
    return math.log(1.0 - 2.0 ** (-5.0 - h))


def _ret_scan_kernel(q_ref, k_ref, v_ref, g_ref, a_ref, km_ref, vm_ref, y_ref, s_ref, s_scr, *,
                     n_heads, dk, dv, chunk, n_meta, n_chunks):
    c = pl.program_id(1)

    @pl.when(c == 0)
    def _():
        mpos = lax.broadcasted_iota(jnp.int32, (n_meta, dk), 0).astype(F32)
        for h in range(n_heads):
            lg = _ret_log_gamma(h)
            kd = (km_ref[:, h * dk:(h + 1) * dk].astype(F32) * jnp.exp(lg * (n_meta - 1.0 - mpos))).astype(BF16)
            s_scr[h] = _dot_tn(kd, vm_ref[:, h * dv:(h + 1) * dv].astype(BF16))

    rel = (lax.broadcasted_iota(jnp.int32, (chunk, chunk), 0)
           - lax.broadcasted_iota(jnp.int32, (chunk, chunk), 1)).astype(F32)
    pos_k = lax.broadcasted_iota(jnp.int32, (chunk, dk), 0).astype(F32)
    pos_v = lax.broadcasted_iota(jnp.int32, (chunk, dv), 0).astype(F32)
    for h in range(n_heads):
        lg = _ret_log_gamma(h)
        ks = slice(h * dk, (h + 1) * dk)
        vs = slice(h * dv, (h + 1) * dv)
        q = q_ref[:, ks]
        k = k_ref[:, ks]
        v = v_ref[:, vs]
        s = s_scr[h]
        decay = jnp.where(rel >= 0, jnp.exp(lg * jnp.maximum(rel, 0.0)), 0.0)
        inner = (_dot_nt(q, k) * decay).astype(BF16)
        o = _dot(inner, v) + _dot(q, s.astype(BF16)) * jnp.exp(lg * (pos_v + 1.0))
        kd = (k.astype(F32) * jnp.exp(lg * (chunk - 1.0 - pos_k))).astype(BF16)
        s_scr[h] = math.exp(lg * chunk) * s + _dot_tn(kd, v)
        mu = jnp.mean(o, axis=-1, keepdims=True)
        dlt = o - mu
        var = jnp.mean(dlt * dlt, axis=-1, keepdims=True)
        gates = g_ref[:, vs].astype(F32) * a_ref[:, vs].astype(F32)
        y_ref[:, vs] = (dlt * lax.rsqrt(var + LN_EPS) * gates).astype(y_ref.dtype)

    @pl.when(c == n_chunks - 1)
    def _():
        s_ref[0, 0] = s_scr[...]


def _ret_scan(qk, vgq, rest, qk_s, vgq_s, *, batch, seq, n_heads, dk, dv, n_meta, meta_row0, depth):
    d = n_heads * dk
    chunk = _pick_tile(seq, RET_CHUNK, BF16_SUBLANES)
    n_chunks = seq // chunk
    assert meta_row0 % n_meta == 0
    mblk = meta_row0 // n_meta
    row = lambda b, c: b * n_chunks + c
    return pl.pallas_call(
        functools.partial(_ret_scan_kernel, n_heads=n_heads, dk=dk, dv=dv, chunk=chunk,
                          n_meta=n_meta, n_chunks=n_chunks),
        grid=(batch, n_chunks),
        in_specs=[
            pl.BlockSpec((chunk, d), lambda b, c: (row(b, c), 0)),
            pl.BlockSpec((chunk, d), lambda b, c: (row(b, c), 1)),
            pl.BlockSpec((chunk, d), lambda b, c: (row(b, c), 0)),
            pl.BlockSpec((chunk, d), lambda b, c: (row(b, c), 1)),
            pl.BlockSpec((chunk, d), lambda b, c: (row(b, c), 2)),
            pl.BlockSpec((n_meta, d), lambda b, c: (mblk, 1)),
            pl.BlockSpec((n_meta, d), lambda b, c: (mblk, 0)),
        ],
        out_specs=[
            pl.BlockSpec((chunk, d), lambda b, c: (row(b, c), 0)),
            pl.BlockSpec((1, 1, n_heads, dk, dv), lambda b, c: (0, b, 0, 0, 0)),
        ],
        out_shape=[
            jax.ShapeDtypeStruct((batch * seq, d), BF16),
            jax.ShapeDtypeStruct((depth, batch, n_heads, dk, dv), F32),
        ],
        scratch_shapes=[pltpu.VMEM((n_heads, dk, dv), F32)],
        compiler_params=_params("arbitrary", "arbitrary"),
        name="ret_scan",
    )(qk, qk, vgq, vgq, rest, qk_s, vgq_s)


def _split_dot(m01, x):
    hi = x.astype(BF16)
    lo = (x - hi.astype(F32)).astype(BF16)
    return _dot(m01, hi) + _dot(m01, lo)


def _hgrn_scan_kernel(q_ref, f_ref, v_ref, g_ref, a_ref, fm_ref, vm_ref, gn_ref, *rest,
                      n_heads, dk, dv, chunk, n_meta, n_chunks, n_levels, n_small, decode_per_step, decode_dims):
    if decode_per_step:
        dec_in, rest = rest[:6], rest[6:]
        (y_ref, s_ref), dec_out, rest = rest[:2], rest[2:5], rest[5:]
    else:
        (y_ref, s_ref), rest = rest[:2], rest[2:]
    st_scr, mask_scr, sums_scr, logsum_scr = rest
    bi = pl.program_id(0)
    c = pl.program_id(1)
    row = lax.broadcasted_iota(jnp.int32, (chunk, chunk), 0)
    col = lax.broadcasted_iota(jnp.int32, (chunk, chunk), 1)

    @pl.when((bi == 0) & (c == 0))
    def _():
        mask_scr[0] = jnp.where(row == col, 1.0, 0.0).astype(BF16)
        sums_scr[0:chunk] = jnp.where(row >= col, 1.0, 0.0).astype(BF16)
        for l in range(1, n_levels + 1):
            same_block = (row >> l) == (col >> l)
            pair = (((row >> (l - 1)) & 1) == 1) & (((col >> (l - 1)) & 1) == 0)
            mask_scr[l] = jnp.where(same_block & pair, 1.0, 0.0).astype(BF16)
            if l <= n_small:
                bnd = ((row >> l) << l) + ((1 << (l - 1)) - 1)
                second = ((row >> (l - 1)) & 1) == 1
                between = (second & (col > bnd) & (col <= row)) | (~second & (col > row) & (col <= bnd))
                sums_scr[l * chunk:(l + 1) * chunk] = jnp.where(between, 1.0, 0.0).astype(BF16)

    @pl.when(c == 0)
    def _():
        mr = lax.broadcasted_iota(jnp.int32, (n_meta, n_meta), 0)
        mc = lax.broadcasted_iota(jnp.int32, (n_meta, n_meta), 1)
        later = jnp.where(mc > mr, 1.0, 0.0).astype(BF16)

        def meta_body(h, carry):
            ks = pl.ds(pl.multiple_of(h * dk, dk), dk)
            vs = pl.ds(pl.multiple_of(h * dv, dv), dv)
            f = fm_ref[:, ks]
            tail = _split_dot(later, jnp.log(f))
            kd = ((1.0 - f) * jnp.exp(tail)).astype(BF16)
            st_scr[h] = _dot_tn(vm_ref[:, vs].astype(BF16), kd)
            return carry

        lax.fori_loop(0, n_heads, meta_body, 0)

    logf = jnp.log(f_ref[...])
    logf_hi = logf.astype(BF16)
    logf_lo = (logf - logf_hi.astype(F32)).astype(BF16)
    logsum_scr[0:chunk] = _dot(sums_scr[0:chunk], logf_hi) + _dot(sums_scr[0:chunk], logf_lo)
    if n_small:
        logsum_scr[chunk:] = _dot(sums_scr[chunk:], logf_hi)

    def head_body(h):
        ks = pl.ds(pl.multiple_of(h * dk, dk), dk)
        vs = pl.ds(pl.multiple_of(h * dv, dv), dv)
        k = 1.0 - f_ref[:, ks]
        kb = k.astype(BF16)
        b = logsum_scr[0:chunk, ks]
        qb = q_ref[:, ks]
        q = qb.astype(F32)
        v = v_ref[:, vs]
        att = mask_scr[0] * _dot_nt(qb, kb).astype(BF16)
        for l in range(1, n_levels + 1):
            m = 1 << l
            half = m >> 1
            if l <= n_small:
                e = jnp.exp(logsum_scr[l * chunk:(l + 1) * chunk, ks]).astype(BF16)
                prod = _dot_nt(qb * e, kb * e)
            else:
                q_rows, k_rows = [], []
                zero = jnp.zeros((half, dk), F32)
                for p in range(chunk // m):
                    r = b[p * m + half - 1:p * m + half, :]
                    lo = slice(p * m, p * m + half)
                    hi = slice(p * m + half, (p + 1) * m)
                    k_rows += [k[lo] * jnp.exp(r - b[lo]), zero]
                    q_rows += [zero, q[hi] * jnp.exp(b[hi] - r)]
                prod = _dot_nt(jnp.concatenate(q_rows, axis=0).astype(BF16),
                               jnp.concatenate(k_rows, axis=0).astype(BF16))
            prod = prod.astype(BF16)
            att = att + (prod if m == chunk else mask_scr[l] * prod)
        st = st_scr[h]
        o = _dot(att, v) + _dot_nt((q * jnp.exp(b)).astype(BF16), st.astype(BF16))
        btot = b[chunk - 1:chunk, :]
        kd = (k * jnp.exp(btot - b)).astype(BF16)
        st_scr[h] = jnp.exp(btot) * st + _dot_tn(v, kd)
        ms = jnp.mean(o * o, axis=-1, keepdims=True)
        gates = g_ref[:, vs].astype(F32) * a_ref[:, vs].astype(F32)
        y_ref[:, vs] = (o * lax.rsqrt(ms + LN_EPS) * gn_ref[:, vs] * gates).astype(y_ref.dtype)

    n_trips = decode_per_step or n_heads // HGRN_HEAD_UNROLL
    heads_per_trip = n_heads // n_trips

    def trip(t, carry):
        for hh in range(heads_per_trip):
            head_body(t * heads_per_trip + hh)
        if decode_per_step:
            _decode_one(t, *dec_in[:4], gn_ref, *dec_in[4:], *dec_out, **decode_dims)
        return carry

    lax.fori_loop(0, n_trips, trip, 0)

    @pl.when(c == n_chunks - 1)
    def _():
        for h in range(n_heads):
            s_ref[0, 0, h] = st_scr[h].T


def _decode_fits_scan(n_dec, batch, seq, n_heads):
    steps = batch * (seq // _pick_tile(seq, HGRN_CHUNK, BF16_SUBLANES))
    per = n_dec // steps
    return per if per and per * steps == n_dec and n_heads % per == 0 else 0


def _hgrn_scan(vgq, fg, rest, fg_s, rest_s, gn, *, batch, seq, n_heads, dk, dv, n_meta, meta_row0, depth,
               decode=None):
    d = n_heads * dk
    chunk = _pick_tile(seq, HGRN_CHUNK, BF16_SUBLANES)
    assert chunk & (chunk - 1) == 0 and chunk >= 8
    n_levels = chunk.bit_length() - 1
    n_small = min(n_levels, F32_SUBLANES.bit_length() - 1)
    n_chunks = seq // chunk
    mblk = meta_row0 // n_meta
    row = lambda b, c: b * n_chunks + c
    in_specs = [
        pl.BlockSpec((chunk, d), lambda b, c: (row(b, c), 2)),
        pl.BlockSpec((chunk, d), lambda b, c: (row(b, c), 0)),
        pl.BlockSpec((chunk, d), lambda b, c: (row(b, c), 0)),
        pl.BlockSpec((chunk, d), lambda b, c: (row(b, c), 1)),
        pl.BlockSpec((chunk, d), lambda b, c: (row(b, c), 3)),
        pl.BlockSpec((n_meta, d), lambda b, c: (mblk, 0)),
        pl.BlockSpec((n_meta, d), lambda b, c: (mblk, 0)),
        pl.BlockSpec((1, d), lambda b, c: (0, 0)),
    ]
    operands = [vgq, fg, rest, rest, rest, fg_s, rest_s, gn]
    out_specs = [
        pl.BlockSpec((chunk, d), lambda b, c: (row(b, c), 0)),
        pl.BlockSpec((1, 1, n_heads, dk, dv), lambda b, c: (0, b, 0, 0, 0)),
    ]
    out_shape = [
        jax.ShapeDtypeStruct((batch * seq, d), BF16),
        jax.ShapeDtypeStruct((depth, batch, n_heads, dk, dv), F32),
    ]
    per, decode_dims = 0, None
    if decode is not None:
        qk_s, vgq_s, state_ret, state_hgrn, per = decode
        _, n_dec, h_ret, dk_r, dv_r = state_ret.shape
        decode_dims = dict(d=d, h_ret=h_ret, dk_r=dk_r, dv_r=dv_r, h_hg=n_heads, dk_h=dk, dv_h=dv)
        as3d = lambda a: a.reshape(a.shape[0], 1, a.shape[1])
        row_spec = lambda w: pl.BlockSpec((per, 1, w), lambda b, c: (row(b, c), 0, 0))
        ret_spec = pl.BlockSpec((1, per, h_ret, dk_r, dv_r), lambda b, c: (0, row(b, c), 0, 0, 0))
        hg_spec = pl.BlockSpec((1, per, n_heads, dk, dv), lambda b, c: (0, row(b, c), 0, 0, 0))
        in_specs += [row_spec(2 * d), row_spec(3 * d), row_spec(d), row_spec(4 * d), ret_spec, hg_spec]
        operands += [as3d(qk_s), as3d(vgq_s), as3d(fg_s), as3d(rest_s), state_ret, state_hgrn]
        out_specs += [row_spec(d), ret_spec, hg_spec]
        out_shape += [jax.ShapeDtypeStruct((n_dec, 1, d), F32), jax.ShapeDtypeStruct(state_ret.shape, F32),
                      jax.ShapeDtypeStruct(state_hgrn.shape, F32)]
    return pl.pallas_call(
        functools.partial(_hgrn_scan_kernel, n_heads=n_heads, dk=dk, dv=dv, chunk=chunk,
                          n_meta=n_meta, n_chunks=n_chunks, n_levels=n_levels, n_small=n_small,
                          decode_per_step=per, decode_dims=decode_dims),
        grid=(batch, n_chunks),
        in_specs=in_specs,
        out_specs=out_specs,
        out_shape=out_shape,
        scratch_shapes=[
            pltpu.VMEM((n_heads, dv, dk), F32),
            pltpu.VMEM((n_levels + 1, chunk, chunk), BF16),
            pltpu.VMEM(((n_small + 1) * chunk, chunk), BF16),
            pltpu.VMEM(((n_small + 1) * chunk, d), F32),
        ],
        compiler_params=_params("arbitrary", "arbitrary"),
        name="hgrn_scan",
    )(*operands)


def _column_bcast(r, n_rows, n_cols):
    return jnp.broadcast_to(r, (n_cols, n_rows)).T


def _decode_kernel(qk_ref, vgq_ref, f_ref, rest_ref, gn_ref, sr_ref, sh_ref,
                   y_ref, sro_ref, sho_ref, *, per_step, **dims):
    for e in range(per_step):
        _decode_one(e, qk_ref, vgq_ref, f_ref, rest_ref, gn_ref, sr_ref, sh_ref, y_ref, sro_ref, sho_ref, **dims)


def _decode_one(e, qk_ref, vgq_ref, f_ref, rest_ref, gn_ref, sr_ref, sh_ref,
                y_ref, sro_ref, sho_ref, *, d, h_ret, dk_r, dv_r, h_hg, dk_h, dv_h):
    qk = qk_ref[e]
    vgq = vgq_ref[e]
    fg = f_ref[e]
    rest = rest_ref[e]
    gn = gn_ref[...]
    out_r = []
    for h in range(h_ret):
        q = qk[:, h * dk_r:(h + 1) * dk_r]
        k = qk[:, d + h * dk_r:d + (h + 1) * dk_r]
        v = vgq[:, h * dv_r:(h + 1) * dv_r]
        s_new = math.exp(_ret_log_gamma(h)) * sr_ref[0, e, h] + _column_bcast(k, dk_r, dv_r) * v
        sro_ref[0, e, h] = s_new
        o = _dot(jnp.broadcast_to(q, (8, dk_r)).astype(BF16), s_new.astype(BF16))[0:1, :]
        mu = jnp.mean(o, axis=-1, keepdims=True)
        dlt = o - mu
        var = jnp.mean(dlt * dlt, axis=-1, keepdims=True)
        gates = vgq[:, d + h * dv_r:d + (h + 1) * dv_r] * rest[:, 2 * d + h * dv_r:2 * d + (h + 1) * dv_r]
        out_r.append(dlt * lax.rsqrt(var + LN_EPS) * gates)
    out_h = []
    for h in range(h_hg):
        ks = slice(h * dk_h, (h + 1) * dk_h)
        vs = slice(h * dv_h, (h + 1) * dv_h)
        f = fg[:, ks]
        q = vgq[:, 2 * d + h * dk_h:2 * d + (h + 1) * dk_h]
        v = rest[:, vs]
        s_new = _column_bcast(f, dk_h, dv_h) * sh_ref[0, e, h] + _column_bcast(1.0 - f, dk_h, dv_h) * v
        sho_ref[0, e, h] = s_new
        o = _dot(jnp.broadcast_to(q, (8, dk_h)).astype(BF16), s_new.astype(BF16))[0:1, :]
        ms = jnp.mean(o * o, axis=-1, keepdims=True)
        gates = rest[:, d + h * dv_h:d + (h + 1) * dv_h] * rest[:, 3 * d + h * dv_h:3 * d + (h + 1) * dv_h]
        out_h.append(o * lax.rsqrt(ms + LN_EPS) * gn[:, vs] * gates)
    y_ref[e] = jnp.concatenate(out_r, axis=1) + jnp.concatenate(out_h, axis=1)


def _decode(qk_s, vgq_s, fg_s, rest_s, gn, state_ret, state_hgrn, *, n_dec):
    depth, _, h_ret, dk_r, dv_r = state_ret.shape
    _, _, h_hg, dk_h, dv_h = state_hgrn.shape
    d = h_ret * dk_r
    rows = qk_s.shape[0]
    per = DECODE_ROWS_PER_STEP if n_dec % DECODE_ROWS_PER_STEP == 0 else 1
    as3d = lambda a: a.reshape(rows, 1, a.shape[1])
    row_spec = lambda w: pl.BlockSpec((per, 1, w), lambda b: (b, 0, 0))
    ret_spec = pl.BlockSpec((1, per, h_ret, dk_r, dv_r), lambda b: (0, b, 0, 0, 0))
    hg_spec = pl.BlockSpec((1, per, h_hg, dk_h, dv_h), lambda b: (0, b, 0, 0, 0))
    y, sr, sh = pl.pallas_call(
        functools.partial(_decode_kernel, per_step=per, d=d, h_ret=h_ret, dk_r=dk_r, dv_r=dv_r,
                          h_hg=h_hg, dk_h=dk_h, dv_h=dv_h),
        grid=(n_dec // per,),
        in_specs=[row_spec(2 * d), row_spec(3 * d), row_spec(d), row_spec(4 * d),
                  pl.BlockSpec((1, d), lambda b: (0, 0)), ret_spec, hg_spec],
        out_specs=[row_spec(d), ret_spec, hg_spec],
        out_shape=[
            jax.ShapeDtypeStruct((n_dec, 1, d), F32),
            jax.ShapeDtypeStruct(state_ret.shape, F32),
            jax.ShapeDtypeStruct(state_hgrn.shape, F32),
        ],
        compiler_params=_params("arbitrary"),
        name="decode",
    )(as3d(qk_s), as3d(vgq_s), as3d(fg_s), as3d(rest_s), gn, state_ret, state_hgrn)
    return y.reshape(n_dec, d), sr, sh


def _wout_ln_kernel(*refs, alpha, n_y):
    y_refs = refs[:n_y]
    x_ref, w_ref, g_ref, b_ref, o_ref = refs[n_y:]
    y = y_refs[0][...].astype(F32)
    for r in y_refs[1:]:
        y = y + r[...].astype(F32)
    m = _dot(y.astype(BF16), w_ref[...])
    o_ref[...] = _layer_norm(alpha * x_ref[...] + m, g_ref[...], b_ref[...])


def _wout_ln(ys, x, w, ln_g, ln_b, *, alpha, name):
    rows, d = x.shape
    tm = _pick_tile(rows, 512, BF16_SUBLANES)
    tile = pl.BlockSpec((tm, d), lambda i: (i, 0))
    vec = pl.BlockSpec((1, d), lambda i: (0, 0))
    return pl.pallas_call(
        functools.partial(_wout_ln_kernel, alpha=alpha, n_y=len(ys)),
        grid=(rows // tm,),
        in_specs=[tile] * len(ys) + [tile, pl.BlockSpec((d, d), lambda i: (0, 0)), vec, vec],
        out_specs=tile,
        out_shape=jax.ShapeDtypeStruct((rows, d), F32),
        compiler_params=_params("arbitrary"),
        name=name,
    )(*ys, x, w, ln_g, ln_b)


def _rotary_tables(pos, dk):
    inv = ROPE_BASE ** (-jnp.arange(0, dk, 2, dtype=F32) / dk)
    ang = pos.astype(F32)[:, None] * inv[None, :]
    cos, sin = jnp.cos(ang), jnp.sin(ang)
    cos_full = jnp.stack([cos, cos], axis=-1).reshape(pos.shape[0], dk)
    sin_signed = jnp.stack([-sin, sin], axis=-1).reshape(pos.shape[0], dk)
    return cos_full, sin_signed


def kernel(x_prompt, x_sample, state_ret, state_hgrn, meta_tokens, ln1_g, ln1_b, ffn1_w_gate, ffn1_w_up, ffn1_w_down, w_in, hgrn_lb_logits, hgrn_norm_g, w_out, ln2_g, ln2_b, ffn2_w_gate, ffn2_w_up, ffn2_w_down, ln3_g, ln3_b):
    batch, seq, d = x_prompt.shape
    n_dec, dec_seq, _ = x_sample.shape
    depth, _, h_ret, dk_r, dv_r = state_ret.shape
    _, _, h_hg, dk_h, dv_h = state_hgrn.shape
    n_meta = meta_tokens.shape[0]
    assert depth == 1 and dec_seq == 1
    alpha = (2.0 * depth) ** 0.25
    layer = 0

    ffn1 = (ffn1_w_gate[layer], ffn1_w_up[layer], ffn1_w_down[layer])
    ffn2 = (ffn2_w_gate[layer], ffn2_w_up[layer], ffn2_w_down[layer])
    w_in_b = w_in[layer]
    w_out_b = w_out[layer].astype(BF16)
    vec = lambda a: a[layer].reshape(1, d)
    lb = jnp.cumsum(jax.nn.softmax(hgrn_lb_logits.astype(F32), axis=0), axis=0)[layer].reshape(1, d)
    gn = vec(hgrn_norm_g)

    tm = _pick_tile(seq, ROW_TILE, BF16_SUBLANES)
    n_tiles = batch * seq // tm
    small_quantum = BF16_SUBLANES * n_tiles
    n_small = -(-(n_dec + n_meta) // small_quantum) * small_quantum
    n_pad = n_small - n_dec - n_meta
    xp = x_prompt.reshape(batch * seq, d)
    xs = jnp.concatenate([x_sample.reshape(n_dec, d), meta_tokens.astype(x_prompt.dtype),
                          jnp.zeros((n_pad, d), x_prompt.dtype)], axis=0)
    cos_p, sin_p = _rotary_tables(n_meta + jnp.arange(seq, dtype=jnp.int32), dk_r)
    pos_s = jnp.concatenate([jnp.full((n_dec,), PAST_LEN, jnp.int32), jnp.arange(n_meta, dtype=jnp.int32),
                             jnp.zeros((n_pad,), jnp.int32)])
    cos_s, sin_s = _rotary_tables(pos_s, dk_r)

    x1p, x1s, x1pb, x1sb = _ffn_ln(xp, xs, *ffn1, vec(ln1_g), vec(ln1_b), tm=tm, alpha=alpha, emit_bf16=True,
                                   name="ffn1")
    (qk_p, qk_s), (vgq_p, vgq_s), (fg_p, fg_s), (rest_p, rest_s) = _project_all(
        x1pb, x1sb, w_in_b, d, tm, (cos_p, sin_p, cos_s, sin_s), lb, dk_r)

    yr_p, state_ret_prompt = _ret_scan(qk_p, vgq_p, rest_p, qk_s, vgq_s, batch=batch, seq=seq, n_heads=h_ret,
                                       dk=dk_r, dv=dv_r, n_meta=n_meta, meta_row0=n_dec, depth=depth)
    hgrn_args = dict(batch=batch, seq=seq, n_heads=h_hg, dk=dk_h, dv=dv_h, n_meta=n_meta, meta_row0=n_dec,
                     depth=depth)
    per = _decode_fits_scan(n_dec, batch, seq, h_hg)
    if per:
        yh_p, state_hgrn_prompt, y_s, state_ret_sample, state_hgrn_sample = _hgrn_scan(
            vgq_p, fg_p, rest_p, fg_s, rest_s, gn, decode=(qk_s, vgq_s, state_ret, state_hgrn, per), **hgrn_args)
        y_s = y_s.reshape(n_dec, d)
    else:
        yh_p, state_hgrn_prompt = _hgrn_scan(vgq_p, fg_p, rest_p, fg_s, rest_s, gn, **hgrn_args)
        y_s, state_ret_sample, state_hgrn_sample = _decode(qk_s, vgq_s, fg_s, rest_s, gn, state_ret,
                                                           state_hgrn, n_dec=n_dec)

    x2p = _wout_ln([yr_p, yh_p], x1p, w_out_b, vec(ln2_g), vec(ln2_b), alpha=alpha, name="wout_prompt")
    x2s = _wout_ln([y_s], x1s[:n_dec], w_out_b, vec(ln2_g), vec(ln2_b), alpha=alpha, name="wout_small")

    x2s = jnp.pad(x2s, ((0, n_small - n_dec), (0, 0)))
    y_prompt, y_small = _ffn_ln(x2p, x2s, *ffn2, vec(ln3_g), vec(ln3_b), tm=tm, alpha=alpha, emit_bf16=False,
                                name="ffn2")
    y_sample = y_small[:n_dec]

    return (y_prompt.reshape(batch, seq, d), y_sample.reshape(n_dec, dec_seq, d), state_ret_prompt,
            state_ret_sample, state_hgrn_prompt, state_hgrn_sample)
```

```python
import functools
import math

import jax
import jax.numpy as jnp
from jax import lax
from jax.experimental import pallas as pl
from jax.experimental.pallas import tpu as pltpu

F32 = jnp.float32
BF16 = jnp.bfloat16

PAST_LEN = 16384
LN_EPS = 1e-5
ROPE_BASE = 10000.0

V7X_VMEM_BYTES = 64 * 1024 * 1024
VMEM_LIMIT_BYTES = V7X_VMEM_BYTES - 8 * 1024 * 1024
LANES = 128
F32_SUBLANES = 8
BF16_SUBLANES = 16

ROW_TILE = 1024
FFN_HIDDEN_TILE = 256
RET_CHUNK = 256
HGRN_CHUNK = 128
DECODE_ROWS_PER_STEP = 2
HGRN_HEAD_UNROLL = 8


def _params(*sem):
    return pltpu.CompilerParams(dimension_semantics=sem, vmem_limit_bytes=VMEM_LIMIT_BYTES)


def _dot(a, b):
    return jnp.dot(a, b, preferred_element_type=F32)


def _dot_nt(a, b):
    return lax.dot_general(a, b, (((1,), (1,)), ((), ())), preferred_element_type=F32)


def _dot_tn(a, b):
    return lax.dot_general(a, b, (((0,), (0,)), ((), ())), preferred_element_type=F32)


def _sigmoid(x):
    return 0.5 * jnp.tanh(0.5 * x) + 0.5


def _silu(x):
    return x * _sigmoid(x)


def _layer_norm(y, g, b):
    mu = jnp.mean(y, axis=-1, keepdims=True)
    d = y - mu
    var = jnp.mean(d * d, axis=-1, keepdims=True)
    return d * lax.rsqrt(var + LN_EPS) * g + b


def _pick_tile(n, target, quantum):
    if n <= target:
        return n
    t = (target // quantum) * quantum
    while t > quantum and n % t:
        t -= quantum
    assert n % t == 0, (n, target, quantum)
    return t


def _ffn_ln_kernel(x_ref, xs_ref, g_ref, b_ref, *rest, alpha, n_main, has_tail, emit_bf16):
    main_w, rest = rest[:3], rest[3:]
    tail_w, rest = (rest[:3], rest[3:]) if has_tail else (None, rest)
    if emit_bf16:
        o_ref, os_ref, ob_ref, osb_ref, xb_scr = rest
    else:
        (o_ref, os_ref, xb_scr), ob_ref, osb_ref = rest, None, None
    tm = x_ref.shape[0]
    j = pl.program_id(1)
    n_steps = n_main + int(has_tail)

    @pl.when(j == 0)
    def _():
        for src, acc_ref, rows in ((x_ref, o_ref, slice(0, tm)), (xs_ref, os_ref, slice(tm, None))):
            x = src[...]
            xb_scr[rows] = x.astype(BF16)
            acc_ref[...] = (2.0 * alpha) * x

    def accumulate(wg_ref, wu_ref, wd_ref):
        xb = xb_scr[...]
        gate = _dot(xb, wg_ref[...].astype(BF16))
        up = _dot(xb, wu_ref[...].astype(BF16))
        h = (_silu(gate) * up).astype(BF16)
        part = _dot(h, wd_ref[...].astype(BF16))
        o_ref[...] += part[:tm]
        os_ref[...] += part[tm:]

    if has_tail:
        pl.when(j < n_main)(functools.partial(accumulate, *main_w))
        pl.when(j == n_main)(functools.partial(accumulate, *tail_w))
    else:
        accumulate(*main_w)

    @pl.when(j == n_steps - 1)
    def _():
        for acc_ref, bf_ref in ((o_ref, ob_ref), (os_ref, osb_ref)):
            y = _layer_norm(0.5 * acc_ref[...], g_ref[...], b_ref[...])
            acc_ref[...] = y
            if emit_bf16:
                bf_ref[...] = y.astype(BF16)


def _ffn_ln(x, xs, wg, wu, wd, ln_g, ln_b, *, tm, alpha, emit_bf16, name):
    rows, d = x.shape
    f = wg.shape[1]
    n_tiles = rows // tm
    ts = xs.shape[0] // n_tiles
    assert tm * n_tiles == rows and ts * n_tiles == xs.shape[0] and ts % BF16_SUBLANES == 0
    tf = min(FFN_HIDDEN_TILE, f)
    n_main = f // tf
    tail = f - n_main * tf
    n_steps = n_main + int(tail > 0)
    once = dict(pipeline_mode=pl.Buffered(1))
    vec = pl.BlockSpec((1, d), lambda i, j: (0, 0))
    main_j = lambda j: jnp.minimum(j, n_main - 1)
    in_specs = [
        pl.BlockSpec((tm, d), lambda i, j: (i, 0)), pl.BlockSpec((ts, d), lambda i, j: (i, 0)), vec, vec,
        pl.BlockSpec((d, tf), lambda i, j: (0, main_j(j))),
        pl.BlockSpec((d, tf), lambda i, j: (0, main_j(j))),
        pl.BlockSpec((tf, d), lambda i, j: (main_j(j), 0)),
    ]
    operands = [x, xs, ln_g, ln_b, wg, wu, wd]
    if tail:
        in_specs += [pl.BlockSpec((d, tail), lambda i, j: (0, 0), **once),
                     pl.BlockSpec((d, tail), lambda i, j: (0, 0), **once),
                     pl.BlockSpec((tail, d), lambda i, j: (0, 0), **once)]
        operands += [wg[:, n_main * tf:], wu[:, n_main * tf:], wd[n_main * tf:, :]]
    dtypes = (F32, BF16) if emit_bf16 else (F32,)
    out_shape, out_specs = [], []
    for dt in dtypes:
        out_shape += [jax.ShapeDtypeStruct((rows, d), dt), jax.ShapeDtypeStruct(xs.shape, dt)]
        out_specs += [pl.BlockSpec((tm, d), lambda i, j: (i, 0), **once),
                      pl.BlockSpec((ts, d), lambda i, j: (i, 0), **once)]
    return pl.pallas_call(
        functools.partial(_ffn_ln_kernel, alpha=alpha, n_main=n_main, has_tail=tail > 0, emit_bf16=emit_bf16),
        grid=(n_tiles, n_steps),
        in_specs=in_specs,
        out_specs=out_specs,
        out_shape=out_shape,
        scratch_shapes=[pltpu.VMEM((tm + ts, d), BF16)],
        compiler_params=_params("arbitrary", "arbitrary"),
        name=name,
    )(*operands)


def _apply_epilogue(kind, acc):
    if kind == "silu":
        return _silu(acc)
    if kind == "sigmoid":
        return _sigmoid(acc)
    assert kind == "none", kind
    return acc


def _proj_kernel(x_ref, xs_ref, w_ref, *rest, epilogues, tiles_per_group, dk, k_scale):
    o_ref, os_ref, wb_scr = rest[-3:]
    tm = x_ref.shape[0]

    @pl.when(pl.program_id(1) == 0)
    def _():
        wb_scr[...] = w_ref[...].astype(BF16)

    acc = _dot(jnp.concatenate([x_ref[...], xs_ref[...]], axis=0), wb_scr[...])
    group = pl.program_id(0) // tiles_per_group
    if epilogues == ("rot_q", "rot_k"):
        cos = jnp.concatenate([rest[0][...], rest[2][...]], axis=0)
        sin = jnp.concatenate([rest[1][...], rest[3][...]], axis=0)
        scale = jnp.where(group == 0, 1.0, k_scale).astype(F32)
        even = (lax.broadcasted_iota(jnp.int32, cos.shape, 1) & 1) == 0
        for hh in range(acc.shape[1] // dk):
            cols = slice(hh * dk, (hh + 1) * dk)
            a = acc[:, cols]
            swapped = jnp.where(even, pltpu.roll(a, dk - 1, 1), pltpu.roll(a, 1, 1))
            res = (a * cos + swapped * sin) * scale
            o_ref[:, cols] = res[:tm].astype(o_ref.dtype)
            os_ref[:, cols] = res[tm:].astype(os_ref.dtype)
        return
    if epilogues == ("fgate",):
        lb = rest[0][...]
        res = lb + (1.0 - lb) * _sigmoid(acc)
    else:
        kinds = sorted(set(epilogues))
        res = _apply_epilogue(kinds[0], acc)
        for kind in kinds[1:]:
            hit = functools.reduce(jnp.logical_or, [group == g for g, e in enumerate(epilogues) if e == kind])
            res = jnp.where(hit, _apply_epilogue(kind, acc), res)
    o_ref[...] = res[:tm].astype(o_ref.dtype)
    os_ref[...] = res[tm:].astype(os_ref.dtype)


def _proj(xb, xsb, w, col0, gw, *, tm, epilogues, out_dtype, name, extras=(), dk=0, k_scale=1.0):
    rows, d = xb.shape
    n_tiles = rows // tm
    ts = xsb.shape[0] // n_tiles
    assert tm * n_tiles == rows and ts * n_tiles == xsb.shape[0]
    ncols = gw * len(epilogues)
    tn = _pick_tile(gw, 1024, max(LANES, dk))
    assert col0 % tn == 0
    in_specs = [
        pl.BlockSpec((tm, d), lambda n, m: (m, 0)),
        pl.BlockSpec((ts, d), lambda n, m: (m, 0)),
        pl.BlockSpec((d, tn), lambda n, m: (0, col0 // tn + n)),
    ]
    if epilogues[0].startswith("rot"):
        tab_blocks = extras[0].shape[0] // tm
        assert tab_blocks * tm == extras[0].shape[0]
        in_specs += [pl.BlockSpec((tm, dk), lambda n, m: (m % tab_blocks, 0))] * 2
        in_specs += [pl.BlockSpec((ts, dk), lambda n, m: (m, 0))] * 2
    elif epilogues == ("fgate",):
        in_specs += [pl.BlockSpec((1, tn), lambda n, m: (0, n))]
    return pl.pallas_call(
        functools.partial(_proj_kernel, epilogues=epilogues, tiles_per_group=gw // tn, dk=dk, k_scale=k_scale),
        grid=(ncols // tn, n_tiles),
        in_specs=in_specs,
        out_specs=[pl.BlockSpec((tm, tn), lambda n, m: (m, n)), pl.BlockSpec((ts, tn), lambda n, m: (m, n))],
        out_shape=[jax.ShapeDtypeStruct((rows, ncols), out_dtype),
                   jax.ShapeDtypeStruct((xsb.shape[0], ncols), F32)],
        scratch_shapes=[pltpu.VMEM((d, tn), BF16)],
        compiler_params=_params("arbitrary", "arbitrary"),
        name=name,
    )(xb, xsb, w, *extras)


def _project_all(xb, xsb, w_in, d, tm, tables, lb, dk_ret):
    qk = _proj(xb, xsb, w_in, 0, d, tm=tm, epilogues=("rot_q", "rot_k"), out_dtype=BF16, extras=tables,
               dk=dk_ret, k_scale=dk_ret ** -0.5, name="proj_qk")
    vgq = _proj(xb, xsb, w_in, 2 * d, d, tm=tm, epilogues=("none", "silu", "silu"), out_dtype=BF16,
                name="proj_vgq")
    fg = _proj(xb, xsb, w_in, 5 * d, d, tm=tm, epilogues=("fgate",), out_dtype=F32, extras=(lb,), name="proj_f")
    rest = _proj(xb, xsb, w_in, 6 * d, d, tm=tm, epilogues=("none", "silu", "sigmoid", "sigmoid"),
                 out_dtype=BF16, name="proj_rest")
    return qk, vgq, fg, rest


def _ret_log_gamma(h):
    return math.log(1.0 - 2.0 ** (-5.0 - h))


def _ret_scan_kernel(q_ref, k_ref, v_ref, g_ref, a_ref, km_ref, vm_ref, y_ref, s_ref,
                     s_scr, decay_scr, kdec_scr, odec_scr, *,
                     n_heads, dk, dv, chunk, n_meta, n_chunks):
    c = pl.program_id(1)

    @pl.when(c == 0)
    def _():
        mpos = lax.broadcasted_iota(jnp.int32, (n_meta, dk), 0).astype(F32)
        for h in range(n_heads):
            lg = _ret_log_gamma(h)
            kd = (km_ref[:, h * dk:(h + 1) * dk].astype(F32) * jnp.exp(lg * (n_meta - 1.0 - mpos))).astype(BF16)
            s_scr[h] = _dot_tn(kd, vm_ref[:, h * dv:(h + 1) * dv].astype(BF16))

    @pl.when((pl.program_id(0) == 0) & (c == 0))
    def _():
        rel = (lax.broadcasted_iota(jnp.int32, (chunk, chunk), 0)
               - lax.broadcasted_iota(jnp.int32, (chunk, chunk), 1)).astype(F32)
        pos_k = lax.broadcasted_iota(jnp.int32, (chunk, dk), 0).astype(F32)
        pos_v = lax.broadcasted_iota(jnp.int32, (chunk, dv), 0).astype(F32)
        for h in range(n_heads):
            lg = _ret_log_gamma(h)
            decay_scr[h] = jnp.where(rel >= 0, jnp.exp(lg * jnp.maximum(rel, 0.0)), 0.0)
            kdec_scr[h] = jnp.exp(lg * (chunk - 1.0 - pos_k))
            odec_scr[h] = jnp.exp(lg * (pos_v + 1.0))

    for h in range(n_heads):
        lg = _ret_log_gamma(h)
        ks = slice(h * dk, (h + 1) * dk)
        vs = slice(h * dv, (h + 1) * dv)
        q = q_ref[:, ks]
        k = k_ref[:, ks]
        v = v_ref[:, vs]
        s = s_scr[h]
        inner = (_dot_nt(q, k) * decay_scr[h]).astype(BF16)
        o = _dot(inner, v) + _dot(q, s.astype(BF16)) * odec_scr[h]
        kd = (k.astype(F32) * kdec_scr[h]).astype(BF16)
        s_scr[h] = math.exp(lg * chunk) * s + _dot_tn(kd, v)
        mu = jnp.mean(o, axis=-1, keepdims=True)
        dlt = o - mu
        var = jnp.mean(dlt * dlt, axis=-1, keepdims=True)
        gates = g_ref[:, vs].astype(F32) * a_ref[:, vs].astype(F32)
        y_ref[:, vs] = (dlt * lax.rsqrt(var + LN_EPS) * gates).astype(y_ref.dtype)

    @pl.when(c == n_chunks - 1)
    def _():
        s_ref[0, 0] = s_scr[...]


def _ret_scan(qk, vgq, rest, qk_s, vgq_s, *, batch, seq, n_heads, dk, dv, n_meta, meta_row0, depth):
    d = n_heads * dk
    chunk = _pick_tile(seq, RET_CHUNK, BF16_SUBLANES)
    n_chunks = seq // chunk
    assert meta_row0 % n_meta == 0
    mblk = meta_row0 // n_meta
    row = lambda b, c: b * n_chunks + c
    return pl.pallas_call(
        functools.partial(_ret_scan_kernel, n_heads=n_heads, dk=dk, dv=dv, chunk=chunk,
                          n_meta=n_meta, n_chunks=n_chunks),
        grid=(batch, n_chunks),
        in_specs=[
            pl.BlockSpec((chunk, d), lambda b, c: (row(b, c), 0)),
            pl.BlockSpec((chunk, d), lambda b, c: (row(b, c), 1)),
            pl.BlockSpec((chunk, d), lambda b, c: (row(b, c), 0)),
            pl.BlockSpec((chunk, d), lambda b, c: (row(b, c), 1)),
            pl.BlockSpec((chunk, d), lambda b, c: (row(b, c), 2)),
            pl.BlockSpec((n_meta, d), lambda b, c: (mblk, 1)),
            pl.BlockSpec((n_meta, d), lambda b, c: (mblk, 0)),
        ],
        out_specs=[
            pl.BlockSpec((chunk, d), lambda b, c: (row(b, c), 0)),
            pl.BlockSpec((1, 1, n_heads, dk, dv), lambda b, c: (0, b, 0, 0, 0)),
        ],
        out_shape=[
            jax.ShapeDtypeStruct((batch * seq, d), BF16),
            jax.ShapeDtypeStruct((depth, batch, n_heads, dk, dv), F32),
        ],
        scratch_shapes=[pltpu.VMEM((n_heads, dk, dv), F32), pltpu.VMEM((n_heads, chunk, chunk), F32),
                        pltpu.VMEM((n_heads, chunk, dk), F32), pltpu.VMEM((n_heads, chunk, dv), F32)],
        compiler_params=_params("arbitrary", "arbitrary"),
        name="ret_scan",
    )(qk, qk, vgq, vgq, rest, qk_s, vgq_s)


def _split_dot(m01, x):
    hi = x.astype(BF16)
    lo = (x - hi.astype(F32)).astype(BF16)
    return _dot(m01, hi) + _dot(m01, lo)


def _hgrn_scan_kernel(q_ref, f_ref, v_ref, g_ref, a_ref, fm_ref, vm_ref, gn_ref, *rest,
                      n_heads, dk, dv, chunk, n_meta, n_chunks, n_levels, n_small, decode_per_step, decode_dims):
    if decode_per_step:
        dec_in, rest = rest[:6], rest[6:]
        (y_ref, s_ref), dec_out, rest = rest[:2], rest[2:5], rest[5:]
    else:
        (y_ref, s_ref), rest = rest[:2], rest[2:]
    st_scr, mask_scr, sums_scr, logsum_scr = rest
    bi = pl.program_id(0)
    c = pl.program_id(1)
    row = lax.broadcasted_iota(jnp.int32, (chunk, chunk), 0)
    col = lax.broadcasted_iota(jnp.int32, (chunk, chunk), 1)

    @pl.when((bi == 0) & (c == 0))
    def _():
        mask_scr[0] = jnp.where(row == col, 1.0, 0.0).astype(BF16)
        sums_scr[0:chunk] = jnp.where(row >= col, 1.0, 0.0).astype(BF16)
        for l in range(1, n_levels + 1):
            same_block = (row >> l) == (col >> l)
            pair = (((row >> (l - 1)) & 1) == 1) & (((col >> (l - 1)) & 1) == 0)
            mask_scr[l] = jnp.where(same_block & pair, 1.0, 0.0).astype(BF16)
            if l <= n_small:
                bnd = ((row >> l) << l) + ((1 << (l - 1)) - 1)
                second = ((row >> (l - 1)) & 1) == 1
                between = (second & (col > bnd) & (col <= row)) | (~second & (col > row) & (col <= bnd))
                sums_scr[l * chunk:(l + 1) * chunk] = jnp.where(between, 1.0, 0.0).astype(BF16)

    @pl.when(c == 0)
    def _():
        mr = lax.broadcasted_iota(jnp.int32, (n_meta, n_meta), 0)
        mc = lax.broadcasted_iota(jnp.int32, (n_meta, n_meta), 1)
        later = jnp.where(mc > mr, 1.0, 0.0).astype(BF16)

        def meta_body(h, carry):
            ks = pl.ds(pl.multiple_of(h * dk, dk), dk)
            vs = pl.ds(pl.multiple_of(h * dv, dv), dv)
            f = fm_ref[:, ks]
            tail = _split_dot(later, jnp.log(f))
            kd = ((1.0 - f) * jnp.exp(tail)).astype(BF16)
            st_scr[h] = _dot_tn(vm_ref[:, vs].astype(BF16), kd)
            return carry

        lax.fori_loop(0, n_heads, meta_body, 0)

    logf = jnp.log(f_ref[...])
    logf_hi = logf.astype(BF16)
    logf_lo = (logf - logf_hi.astype(F32)).astype(BF16)
    logsum_scr[0:chunk] = _dot(sums_scr[0:chunk], logf_hi) + _dot(sums_scr[0:chunk], logf_lo)
    if n_small:
        logsum_scr[chunk:] = _dot(sums_scr[chunk:], logf_hi)

    def head_body(h):
        ks = pl.ds(pl.multiple_of(h * dk, dk), dk)
        vs = pl.ds(pl.multiple_of(h * dv, dv), dv)
        k = 1.0 - f_ref[:, ks]
        kb = k.astype(BF16)
        b = logsum_scr[0:chunk, ks]
        qb = q_ref[:, ks]
        q = qb.astype(F32)
        v = v_ref[:, vs]
        att = mask_scr[0] * _dot_nt(qb, kb).astype(BF16)
        for l in range(1, n_levels + 1):
            m = 1 << l
            half = m >> 1
            if l <= n_small:
                e = jnp.exp(logsum_scr[l * chunk:(l + 1) * chunk, ks]).astype(BF16)
                prod = _dot_nt(qb * e, kb * e)
            else:
                q_rows, k_rows = [], []
                zero = jnp.zeros((half, dk), F32)
                for p in range(chunk // m):
                    r = b[p * m + half - 1:p * m + half, :]
                    lo = slice(p * m, p * m + half)
                    hi = slice(p * m + half, (p + 1) * m)
                    k_rows += [k[lo] * jnp.exp(r - b[lo]), zero]
                    q_rows += [zero, q[hi] * jnp.exp(b[hi] - r)]
                prod = _dot_nt(jnp.concatenate(q_rows, axis=0).astype(BF16),
                               jnp.concatenate(k_rows, axis=0).astype(BF16))
            prod = prod.astype(BF16)
            att = att + (prod if m == chunk else mask_scr[l] * prod)
        st = st_scr[h]
        o = _dot(att, v) + _dot_nt((q * jnp.exp(b)).astype(BF16), st.astype(BF16))
        btot = b[chunk - 1:chunk, :]
        kd = (k * jnp.exp(btot - b)).astype(BF16)
        st_scr[h] = jnp.exp(btot) * st + _dot_tn(v, kd)
        ms = jnp.mean(o * o, axis=-1, keepdims=True)
        gates = g_ref[:, vs].astype(F32) * a_ref[:, vs].astype(F32)
        y_ref[:, vs] = (o * lax.rsqrt(ms + LN_EPS) * gn_ref[:, vs] * gates).astype(y_ref.dtype)

    n_trips = decode_per_step or n_heads // HGRN_HEAD_UNROLL
    heads_per_trip = n_heads // n_trips

    def trip(t, carry):
        for hh in range(heads_per_trip):
            head_body(t * heads_per_trip + hh)
        if decode_per_step:
            row0 = ((bi * n_chunks + c) * decode_per_step) % F32_SUBLANES
            _decode_one(t, *dec_in[:4], gn_ref, *dec_in[4:], *dec_out, row0=row0, **decode_dims)
        return carry

    lax.fori_loop(0, n_trips, trip, 0)

    @pl.when(c == n_chunks - 1)
    def _():
        for h in range(n_heads):
            s_ref[0, 0, h] = st_scr[h].T


def _decode_fits_scan(n_dec, batch, seq, n_heads):
    steps = batch * (seq // _pick_tile(seq, HGRN_CHUNK, BF16_SUBLANES))
    per = n_dec // steps
    return per if per and per * steps == n_dec and n_heads % per == 0 and F32_SUBLANES % per == 0 else 0


def _hgrn_scan(vgq, fg, rest, fg_s, rest_s, gn, *, batch, seq, n_heads, dk, dv, n_meta, meta_row0, depth,
               decode=None):
    d = n_heads * dk
    chunk = _pick_tile(seq, HGRN_CHUNK, BF16_SUBLANES)
    assert chunk & (chunk - 1) == 0 and chunk >= 8
    n_levels = chunk.bit_length() - 1
    n_small = min(n_levels, F32_SUBLANES.bit_length() - 1)
    n_chunks = seq // chunk
    mblk = meta_row0 // n_meta
    row = lambda b, c: b * n_chunks + c
    in_specs = [
        pl.BlockSpec((chunk, d), lambda b, c: (row(b, c), 2)),
        pl.BlockSpec((chunk, d), lambda b, c: (row(b, c), 0)),
        pl.BlockSpec((chunk, d), lambda b, c: (row(b, c), 0)),
        pl.BlockSpec((chunk, d), lambda b, c: (row(b, c), 1)),
        pl.BlockSpec((chunk, d), lambda b, c: (row(b, c), 3)),
        pl.BlockSpec((n_meta, d), lambda b, c: (mblk, 0)),
        pl.BlockSpec((n_meta, d), lambda b, c: (mblk, 0)),
        pl.BlockSpec((1, d), lambda b, c: (0, 0)),
    ]
    operands = [vgq, fg, rest, rest, rest, fg_s, rest_s, gn]
    out_specs = [
        pl.BlockSpec((chunk, d), lambda b, c: (row(b, c), 0)),
        pl.BlockSpec((1, 1, n_heads, dk, dv), lambda b, c: (0, b, 0, 0, 0)),
    ]
    out_shape = [
        jax.ShapeDtypeStruct((batch * seq, d), BF16),
        jax.ShapeDtypeStruct((depth, batch, n_heads, dk, dv), F32),
    ]
    per, decode_dims = 0, None
    if decode is not None:
        qk_s, vgq_s, state_ret, state_hgrn, per = decode
        _, n_dec, h_ret, dk_r, dv_r = state_ret.shape
        decode_dims = dict(d=d, h_ret=h_ret, dk_r=dk_r, dv_r=dv_r, h_hg=n_heads, dk_h=dk, dv_h=dv)
        tile_spec = lambda w: pl.BlockSpec((F32_SUBLANES, w), lambda b, c: (row(b, c) * per // F32_SUBLANES, 0))
        ret_spec = pl.BlockSpec((1, per, h_ret, dk_r, dv_r), lambda b, c: (0, row(b, c), 0, 0, 0))
        hg_spec = pl.BlockSpec((1, per, n_heads, dk, dv), lambda b, c: (0, row(b, c), 0, 0, 0))
        in_specs += [tile_spec(2 * d), tile_spec(3 * d), tile_spec(d), tile_spec(4 * d), ret_spec, hg_spec]
        operands += [qk_s, vgq_s, fg_s, rest_s, state_ret, state_hgrn]
        out_specs += [pl.BlockSpec((per, 1, d), lambda b, c: (row(b, c), 0, 0)), ret_spec, hg_spec]
        out_shape += [jax.ShapeDtypeStruct((n_dec, 1, d), F32), jax.ShapeDtypeStruct(state_ret.shape, F32),
                      jax.ShapeDtypeStruct(state_hgrn.shape, F32)]
    return pl.pallas_call(
        functools.partial(_hgrn_scan_kernel, n_heads=n_heads, dk=dk, dv=dv, chunk=chunk,
                          n_meta=n_meta, n_chunks=n_chunks, n_levels=n_levels, n_small=n_small,
                          decode_per_step=per, decode_dims=decode_dims),
        grid=(batch, n_chunks),
        in_specs=in_specs,
        out_specs=out_specs,
        out_shape=out_shape,
        scratch_shapes=[
            pltpu.VMEM((n_heads, dv, dk), F32),
            pltpu.VMEM((n_levels + 1, chunk, chunk), BF16),
            pltpu.VMEM(((n_small + 1) * chunk, chunk), BF16),
            pltpu.VMEM(((n_small + 1) * chunk, d), F32),
        ],
        compiler_params=_params("arbitrary", "arbitrary"),
        name="hgrn_scan",
    )(*operands)


def _column_bcast(r, n_rows, n_cols):
    return jnp.broadcast_to(r, (n_cols, n_rows)).T


def _decode_kernel(qk_ref, vgq_ref, f_ref, rest_ref, gn_ref, sr_ref, sh_ref,
                   y_ref, sro_ref, sho_ref, *, per_step, **dims):
    for e in range(per_step):
        _decode_one(e, qk_ref, vgq_ref, f_ref, rest_ref, gn_ref, sr_ref, sh_ref, y_ref, sro_ref, sho_ref, **dims)


def _decode_one(e, qk_ref, vgq_ref, f_ref, rest_ref, gn_ref, sr_ref, sh_ref,
                y_ref, sro_ref, sho_ref, *, d, h_ret, dk_r, dv_r, h_hg, dk_h, dv_h, row0=None):
    load = (lambda r: r[e]) if row0 is None else (lambda r: r[pl.ds(row0 + e, 1), :])
    qk = load(qk_ref)
    vgq = load(vgq_ref)
    fg = load(f_ref)
    rest = load(rest_ref)
    gn = gn_ref[...]
    out_r = []
    for h in range(h_ret):
        q = qk[:, h * dk_r:(h + 1) * dk_r]
        k = qk[:, d + h * dk_r:d + (h + 1) * dk_r]
        v = vgq[:, h * dv_r:(h + 1) * dv_r]
        s_new = math.exp(_ret_log_gamma(h)) * sr_ref[0, e, h] + _column_bcast(k, dk_r, dv_r) * v
        sro_ref[0, e, h] = s_new
        o = _dot(jnp.broadcast_to(q, (8, dk_r)).astype(BF16), s_new.astype(BF16))[0:1, :]
        mu = jnp.mean(o, axis=-1, keepdims=True)
        dlt = o - mu
        var = jnp.mean(dlt * dlt, axis=-1, keepdims=True)
        gates = vgq[:, d + h * dv_r:d + (h + 1) * dv_r] * rest[:, 2 * d + h * dv_r:2 * d + (h + 1) * dv_r]
        out_r.append(dlt * lax.rsqrt(var + LN_EPS) * gates)
    out_h = []
    for h in range(h_hg):
        ks = slice(h * dk_h, (h + 1) * dk_h)
        vs = slice(h * dv_h, (h + 1) * dv_h)
        f = fg[:, ks]
        q = vgq[:, 2 * d + h * dk_h:2 * d + (h + 1) * dk_h]
        v = rest[:, vs]
        s_new = _column_bcast(f, dk_h, dv_h) * sh_ref[0, e, h] + _column_bcast(1.0 - f, dk_h, dv_h) * v
        sho_ref[0, e, h] = s_new
        o = _dot(jnp.broadcast_to(q, (8, dk_h)).astype(BF16), s_new.astype(BF16))[0:1, :]
        ms = jnp.mean(o * o, axis=-1, keepdims=True)
        gates = rest[:, d + h * dv_h:d + (h + 1) * dv_h] * rest[:, 3 * d + h * dv_h:3 * d + (h + 1) * dv_h]
        out_h.append(o * lax.rsqrt(ms + LN_EPS) * gn[:, vs] * gates)
    y_ref[e] = jnp.concatenate(out_r, axis=1) + jnp.concatenate(out_h, axis=1)


def _decode(qk_s, vgq_s, fg_s, rest_s, gn, state_ret, state_hgrn, *, n_dec):
    depth, _, h_ret, dk_r, dv_r = state_ret.shape
    _, _, h_hg, dk_h, dv_h = state_hgrn.shape
    d = h_ret * dk_r
    rows = qk_s.shape[0]
    per = DECODE_ROWS_PER_STEP if n_dec % DECODE_ROWS_PER_STEP == 0 else 1
    as3d = lambda a: a.reshape(rows, 1, a.shape[1])
    row_spec = lambda w: pl.BlockSpec((per, 1, w), lambda b: (b, 0, 0))
    ret_spec = pl.BlockSpec((1, per, h_ret, dk_r, dv_r), lambda b: (0, b, 0, 0, 0))
    hg_spec = pl.BlockSpec((1, per, h_hg, dk_h, dv_h), lambda b: (0, b, 0, 0, 0))
    y, sr, sh = pl.pallas_call(
        functools.partial(_decode_kernel, per_step=per, d=d, h_ret=h_ret, dk_r=dk_r, dv_r=dv_r,
                          h_hg=h_hg, dk_h=dk_h, dv_h=dv_h),
        grid=(n_dec // per,),
        in_specs=[row_spec(2 * d), row_spec(3 * d), row_spec(d), row_spec(4 * d),
                  pl.BlockSpec((1, d), lambda b: (0, 0)), ret_spec, hg_spec],
        out_specs=[row_spec(d), ret_spec, hg_spec],
        out_shape=[
            jax.ShapeDtypeStruct((n_dec, 1, d), F32),
            jax.ShapeDtypeStruct(state_ret.shape, F32),
            jax.ShapeDtypeStruct(state_hgrn.shape, F32),
        ],
        compiler_params=_params("arbitrary"),
        name="decode",
    )(as3d(qk_s), as3d(vgq_s), as3d(fg_s), as3d(rest_s), gn, state_ret, state_hgrn)
    return y.reshape(n_dec, d), sr, sh


def _wout_ln_kernel(*refs, alpha, n_y):
    y_refs = refs[:n_y]
    x_ref, w_ref, g_ref, b_ref, o_ref = refs[n_y:]
    y = y_refs[0][...].astype(F32)
    for r in y_refs[1:]:
        y = y + r[...].astype(F32)
    m = _dot(y.astype(BF16), w_ref[...])
    o_ref[...] = _layer_norm(alpha * x_ref[...] + m, g_ref[...], b_ref[...])


def _wout_ln(ys, x, w, ln_g, ln_b, *, alpha, name):
    rows, d = x.shape
    tm = _pick_tile(rows, 512, BF16_SUBLANES)
    tile = pl.BlockSpec((tm, d), lambda i: (i, 0))
    vec = pl.BlockSpec((1, d), lambda i: (0, 0))
    return pl.pallas_call(
        functools.partial(_wout_ln_kernel, alpha=alpha, n_y=len(ys)),
        grid=(rows // tm,),
        in_specs=[tile] * len(ys) + [tile, pl.BlockSpec((d, d), lambda i: (0, 0)), vec, vec],
        out_specs=tile,
        out_shape=jax.ShapeDtypeStruct((rows, d), F32),
        compiler_params=_params("arbitrary"),
        name=name,
    )(*ys, x, w, ln_g, ln_b)


def _rotary_tables(pos, dk):
    inv = ROPE_BASE ** (-jnp.arange(0, dk, 2, dtype=F32) / dk)
    ang = pos.astype(F32)[:, None] * inv[None, :]
    cos, sin = jnp.cos(ang), jnp.sin(ang)
    cos_full = jnp.stack([cos, cos], axis=-1).reshape(pos.shape[0], dk)
    sin_signed = jnp.stack([-sin, sin], axis=-1).reshape(pos.shape[0], dk)
    return cos_full, sin_signed


def kernel(x_prompt, x_sample, state_ret, state_hgrn, meta_tokens, ln1_g, ln1_b, ffn1_w_gate, ffn1_w_up, ffn1_w_down, w_in, hgrn_lb_logits, hgrn_norm_g, w_out, ln2_g, ln2_b, ffn2_w_gate, ffn2_w_up, ffn2_w_down, ln3_g, ln3_b):
    batch, seq, d = x_prompt.shape
    n_dec, dec_seq, _ = x_sample.shape
    depth, _, h_ret, dk_r, dv_r = state_ret.shape
    _, _, h_hg, dk_h, dv_h = state_hgrn.shape
    n_meta = meta_tokens.shape[0]
    assert depth == 1 and dec_seq == 1
    alpha = (2.0 * depth) ** 0.25
    layer = 0

    ffn1 = (ffn1_w_gate[layer], ffn1_w_up[layer], ffn1_w_down[layer])
    ffn2 = (ffn2_w_gate[layer], ffn2_w_up[layer], ffn2_w_down[layer])
    w_in_b = w_in[layer]
    w_out_b = w_out[layer].astype(BF16)
    vec = lambda a: a[layer].reshape(1, d)
    lb = jnp.cumsum(jax.nn.softmax(hgrn_lb_logits.astype(F32), axis=0), axis=0)[layer].reshape(1, d)
    gn = vec(hgrn_norm_g)

    tm = _pick_tile(seq, ROW_TILE, BF16_SUBLANES)
    n_tiles = batch * seq // tm
    small_quantum = BF16_SUBLANES * n_tiles
    n_small = -(-(n_dec + n_meta) // small_quantum) * small_quantum
    n_pad = n_small - n_dec - n_meta
    xp = x_prompt.reshape(batch * seq, d)
    xs = jnp.concatenate([x_sample.reshape(n_dec, d), meta_tokens.astype(x_prompt.dtype),
                          jnp.zeros((n_pad, d), x_prompt.dtype)], axis=0)
    cos_p, sin_p = _rotary_tables(n_meta + jnp.arange(seq, dtype=jnp.int32), dk_r)
    pos_s = jnp.concatenate([jnp.full((n_dec,), PAST_LEN, jnp.int32), jnp.arange(n_meta, dtype=jnp.int32),
                             jnp.zeros((n_pad,), jnp.int32)])
    cos_s, sin_s = _rotary_tables(pos_s, dk_r)

    x1p, x1s, x1pb, x1sb = _ffn_ln(xp, xs, *ffn1, vec(ln1_g), vec(ln1_b), tm=tm, alpha=alpha, emit_bf16=True,
                                   name="ffn1")
    (qk_p, qk_s), (vgq_p, vgq_s), (fg_p, fg_s), (rest_p, rest_s) = _project_all(
        x1pb, x1sb, w_in_b, d, tm, (cos_p, sin_p, cos_s, sin_s), lb, dk_r)

    yr_p, state_ret_prompt = _ret_scan(qk_p, vgq_p, rest_p, qk_s, vgq_s, batch=batch, seq=seq, n_heads=h_ret,
                                       dk=dk_r, dv=dv_r, n_meta=n_meta, meta_row0=n_dec, depth=depth)
    hgrn_args = dict(batch=batch, seq=seq, n_heads=h_hg, dk=dk_h, dv=dv_h, n_meta=n_meta, meta_row0=n_dec,
                     depth=depth)
    per = _decode_fits_scan(n_dec, batch, seq, h_hg)
    if per:
        yh_p, state_hgrn_prompt, y_s, state_ret_sample, state_hgrn_sample = _hgrn_scan(
            vgq_p, fg_p, rest_p, fg_s, rest_s, gn, decode=(qk_s, vgq_s, state_ret, state_hgrn, per), **hgrn_args)
        y_s = y_s.reshape(n_dec, d)
    else:
        yh_p, state_hgrn_prompt = _hgrn_scan(vgq_p, fg_p, rest_p, fg_s, rest_s, gn, **hgrn_args)
        y_s, state_ret_sample, state_hgrn_sample = _decode(qk_s, vgq_s, fg_s, rest_s, gn, state_ret,
                                                           state_hgrn, n_dec=n_dec)

    x2p = _wout_ln([yr_p, yh_p], x1p, w_out_b, vec(ln2_g), vec(ln2_b), alpha=alpha, name="wout_prompt")
    x2s = _wout_ln([y_s], x1s[:n_dec], w_out_b, vec(ln2_g), vec(ln2_b), alpha=alpha, name="wout_small")

    x2s = jnp.pad(x2s, ((0, n_small - n_dec), (0, 0)))
    y_prompt, y_small = _ffn_ln(x2p, x2s, *ffn2, vec(ln3_g), vec(ln3_b), tm=tm, alpha=alpha, emit_bf16=False,
                                name="ffn2")
    y_sample = y_small[:n_dec]

    return (y_prompt.reshape(batch, seq, d), y_sample.reshape(n_dec, dec_seq, d), state_ret_prompt,
            state_ret_sample, state_hgrn_prompt, state_hgrn_sample)
```
